```python
import math
import jax, jax.numpy as jnp
from jax import lax
import numpy as np

D_MODEL = 1024
BATCH = 4
SEQ = 4096
DEPTH = 4
DEC_BATCH = 128
DEC_SEQ = 1
PAST_LEN = 2048
PAGE_SIZE = 128

NSA_HEADS = 4
NSA_HD = 64
CMP_STRIDE = 16
CMP_BLOCK = 32
CMP_HID = 128
SEL_BLOCK = 64
SEL_TOP_N = 16
WINDOW = 512
POOL_WINDOWS = (2, 4, 8, 16)
POOL_GROUP = 64
POOL_W = 256
POOL_KEEP = 15
CONV_W = 256
CONV_K = 3
DIFF_HEADS = 4
DIFF_QK = 32
DIFF_V = 64
N_BRANCH = 4
BRANCH_W = 256
D_FF = 2816
FFN_K = 3
QBLOCK = 128
EPS = 1e-6

Q_A = NSA_HEADS * NSA_HD
KV_W = 4 * NSA_HD + DIFF_HEADS * 2 * DIFF_QK + DIFF_HEADS * DIFF_V
WIN_W = 2 * NSA_HD
GA_W = 3 * NSA_HEADS
Q_D = DIFF_HEADS * 2 * DIFF_QK
GATE_W = N_BRANCH * D_MODEL
SPLITS = (Q_A, KV_W, WIN_W, GA_W, POOL_W, 3 * CONV_W, Q_D, GATE_W)
PROJ_W = sum(SPLITS)

kernel_name = 'hybrid_nsa_pool_conv_diff_decoder_step'


def rms_norm(x, g):
    xf = x.astype(jnp.float32)
    y = xf * lax.rsqrt(jnp.mean(xf * xf, axis=-1, keepdims=True) + EPS)
    return (y * g.astype(jnp.float32)).astype(x.dtype)


def alibi_slopes(n):
    return jnp.asarray(2.0 ** (-8.0 * np.arange(1, n + 1) / n), dtype=jnp.float32)


def masked_softmax(s, mask):
    s = jnp.where(mask, s.astype(jnp.float32), -jnp.inf)
    m = jnp.max(s, axis=-1, keepdims=True)
    m = jnp.where(jnp.isfinite(m), m, 0.0)
    e = jnp.where(mask, jnp.exp(s - m), 0.0)
    d = jnp.sum(e, axis=-1, keepdims=True)
    return e / jnp.where(d > 0, d, 1.0)


def causal_dwconv(z, prev, w):
    K = w.shape[0]
    T = z.shape[1]
    ext = jnp.concatenate([prev.astype(z.dtype), z], axis=1)
    out = ext[:, 0:T] * w[0]
    for k in range(1, K):
        out = out + ext[:, k:k + T] * w[k]
    return out, ext[:, ext.shape[1] - (K - 1):]


def query_blocks(fn, t_pos, q):
    B, T = q.shape[0], q.shape[1]
    qb = QBLOCK if T % QBLOCK == 0 else T
    nb = T // qb
    if nb == 1:
        return fn(t_pos, q)
    qs = jnp.moveaxis(q.reshape(B, nb, qb, *q.shape[2:]), 1, 0)
    ts = t_pos.reshape(nb, qb)
    outs = lax.map(lambda a: fn(a[0], a[1]), (ts, qs))
    return tuple(jnp.moveaxis(o, 0, 1).reshape(B, T, *o.shape[3:]) for o in outs)


def compress(rows, pe, w1, w2):
    B, Lp, hd = rows.shape
    ratio = CMP_BLOCK // CMP_STRIDE
    n_ch = Lp // CMP_STRIDE
    n_cmp = n_ch - ratio + 1
    ch = rows.reshape(B, n_ch, CMP_STRIDE, hd)
    blk = jnp.concatenate([ch[:, r:r + n_cmp] for r in range(ratio)], axis=2)
    blk = (blk + pe).reshape(B, n_cmp, CMP_BLOCK * hd)
    return jax.nn.gelu(blk @ w1) @ w2


def cmp_sel_overlap(n_cmp, n_sel):
    cs = np.arange(n_cmp)[:, None] * CMP_STRIDE
    ss = np.arange(n_sel)[None, :] * SEL_BLOCK
    ov = np.minimum(cs + CMP_BLOCK, ss + SEL_BLOCK) - np.maximum(cs, ss)
    return jnp.asarray(np.maximum(ov, 0) / CMP_BLOCK, dtype=jnp.float32)


def nsa_global_branches(q, t_pos, k_cmp, v_cmp, k_sel, v_sel, pe, w1, w2):
    B, L, hd = k_cmp.shape
    Lp = -(-L // SEL_BLOCK) * SEL_BLOCK
    pad = lambda a: jnp.pad(a, ((0, 0), (0, Lp - L), (0, 0)))
    kc = compress(pad(k_cmp), pe[0], w1[0], w2[0])
    vc = compress(pad(v_cmp), pe[1], w1[1], w2[1])
    n_cmp = kc.shape[1]
    n_sel = Lp // SEL_BLOCK
    cmp_end = jnp.arange(n_cmp, dtype=jnp.int32) * CMP_STRIDE + (CMP_BLOCK - 1)
    kb = pad(k_sel).reshape(B, n_sel, SEL_BLOCK, hd)
    vb = pad(v_sel).reshape(B, n_sel, SEL_BLOCK, hd)
    overlap = cmp_sel_overlap(n_cmp, n_sel)
    top = min(SEL_TOP_N, n_sel)
    slopes = alibi_slopes(NSA_HEADS)
    scale = NSA_HD ** -0.5
    blk_ids = jnp.arange(n_sel, dtype=jnp.int32)
    offs = jnp.arange(SEL_BLOCK, dtype=jnp.int32)
    gather_blocks = jax.vmap(lambda a, i: a[i])

    def block_fn(t, qb):
        Bq, Tq, H = qb.shape[0], qb.shape[1], qb.shape[2]
        s_c = jnp.einsum('bthd,bnd->bhtn', qb, kc).astype(jnp.float32) * scale
        p_c = masked_softmax(s_c, (cmp_end[None, :] <= t[:, None])[None, None])
        o_cmp = jnp.einsum('bhtn,bnd->bthd', p_c, vc)
        imp = jnp.einsum('bhtn,nj->btj', p_c, overlap)
        valid = blk_ids[None, :] * SEL_BLOCK <= t[:, None]
        cur = (t // SEL_BLOCK)[:, None]
        forced = (blk_ids[None, :] == 0) | (blk_ids[None, :] == cur) | (blk_ids[None, :] == cur - 1)
        score = jnp.where(valid & forced, jnp.inf, jnp.where(valid, imp, -jnp.inf))
        _, idx = lax.top_k(score, top)
        ks = gather_blocks(kb, idx)
        vs = gather_blocks(vb, idx)
        pos = idx[..., None] * SEL_BLOCK + offs
        dist = t[None, :, None, None] - pos
        s_s = (jnp.einsum('bthd,btkrd->bhtkr', qb, ks).astype(jnp.float32) * scale
               - slopes[None, :, None, None, None] * dist[:, None].astype(jnp.float32))
        p_s = masked_softmax(s_s.reshape(Bq, H, Tq, top * SEL_BLOCK),
                             (dist >= 0).reshape(Bq, 1, Tq, top * SEL_BLOCK)).reshape(s_s.shape)
        o_sel = jnp.einsum('bhtkr,btkrd->bthd', p_s, vs)
        return (o_cmp.astype(qb.dtype), o_sel.astype(qb.dtype))

    return query_blocks(block_fn, t_pos, q)


def window_attention(q, kv_ext, t0):
    B, T, H, hd = q.shape
    P = kv_ext.shape[1] - T
    qb = QBLOCK if T % QBLOCK == 0 else T
    nb = T // qb
    span = P + qb
    idx = np.arange(nb)[:, None] * qb + np.arange(span)[None, :]
    kvb = kv_ext[:, idx]
    kb, vb = kvb[..., :hd], kvb[..., hd:]
    tq = t0 + np.arange(T).reshape(nb, qb)
    pk = t0 - P + idx
    dist_np = tq[:, :, None] - pk[:, None, :]
    mask = jnp.asarray((dist_np >= 0) & (dist_np < WINDOW) & (pk[:, None, :] >= 0))
    dist = jnp.asarray(dist_np, dtype=jnp.float32)
    slopes = alibi_slopes(NSA_HEADS)
    s = (jnp.einsum('bnqhd,bnkd->bnhqk', q.reshape(B, nb, qb, H, hd), kb).astype(jnp.float32) * (hd ** -0.5)
         - slopes[None, None, :, None, None] * dist[None, :, None])
    p = masked_softmax(s, mask[None, :, None])
    o = jnp.einsum('bnhqk,bnkd->bnqhd', p, vb)
    return o.reshape(B, T, H, hd).astype(q.dtype)


def diff_attention(q, t_pos, k, v, lam, lam_init, norm_g):
    L = k.shape[1]
    s_pos = jnp.arange(L, dtype=jnp.int32)
    slopes = alibi_slopes(DIFF_HEADS)
    scale = DIFF_QK ** -0.5

    def block_fn(t, qb):
        dist = t[:, None] - s_pos[None, :]
        s = (jnp.einsum('bthcd,bshcd->bhcts', qb, k).astype(jnp.float32) * scale
             - slopes[None, :, None, None, None] * dist.astype(jnp.float32)[None, None, None])
        p = masked_softmax(s, (dist >= 0)[None, None, None])
        a = p[:, :, 0] - lam * p[:, :, 1]
        o = jnp.einsum('bhts,bshd->bthd', a, v)
        o = rms_norm(o, norm_g) * (1.0 - lam_init)
        return (o.astype(qb.dtype),)

    return query_blocks(block_fn, t_pos, q)[0]


def pool_mixer(u, prev, t0, w, scale):
    B, T, C = u.shape
    P = prev.shape[1]
    raw = jnp.concatenate([prev.astype(u.dtype), u], axis=1)
    ext = raw.astype(jnp.float32)
    cs = jnp.concatenate([jnp.zeros((B, 1, C), jnp.float32), lax.cumsum(ext, axis=1)], axis=1)
    pos = t0 + jnp.arange(T, dtype=jnp.int32)
    outs = []
    for gi, win in enumerate(POOL_WINDOWS):
        sl = slice(gi * POOL_GROUP, (gi + 1) * POOL_GROUP)
        tot = cs[:, P + 1:P + 1 + T, sl] - cs[:, P + 1 - win:P + 1 - win + T, sl]
        cnt = jnp.minimum(win, pos + 1).astype(jnp.float32)[None, :, None]
        outs.append(tot / cnt)
    pooled = jnp.concatenate(outs, axis=-1) - u.astype(jnp.float32)
    y = jnp.einsum('btgc,gcd->btgd', pooled.reshape(B, T, len(POOL_WINDOWS), POOL_GROUP), w)
    y = y.reshape(B, T, C) * scale
    return y.astype(u.dtype), raw[:, raw.shape[1] - POOL_KEEP:]


def trunk_layer(l, x, c, kv_past, win_prev, pool_prev, conv_prev, ffn_prev, t0, p):
    B, T, _ = x.shape
    mod = jax.nn.silu(c) @ p['ada_w'] + p['ada_b']
    sh1, sc1, gt1, sh2, sc2, gt2 = [m[:, None, :] for m in jnp.split(mod, 6, axis=-1)]
    h = rms_norm(x, p['norm_g'][0]) * (1 + sc1) + sh1
    proj = h @ p['w_in']
    cuts = [int(v) for v in np.cumsum(SPLITS)[:-1]]
    q_a, kv_new, win_new, g_a, u_b, cv, q_d, gates = jnp.split(proj, cuts, axis=-1)
    t_pos = t0 + jnp.arange(T, dtype=jnp.int32)
    kv = kv_new if kv_past is None else jnp.concatenate([kv_past.astype(kv_new.dtype), kv_new], axis=1)
    L = kv.shape[1]
    a = NSA_HD

    qa = q_a.reshape(B, T, NSA_HEADS, NSA_HD)
    o_cmp, o_sel = nsa_global_branches(qa, t_pos, kv[..., 0:a], kv[..., a:2 * a], kv[..., 2 * a:3 * a],
                                       kv[..., 3 * a:4 * a], p['cmp_pe'], p['cmp_w1'], p['cmp_w2'])
    win_ext = jnp.concatenate([win_prev.astype(win_new.dtype), win_new], axis=1)
    o_win = window_attention(qa, win_ext, t0)
    ga = jax.nn.sigmoid(g_a.reshape(B, T, NSA_HEADS, 3))
    o_a = (ga[..., 0:1] * o_cmp + ga[..., 1:2] * o_sel + ga[..., 2:3] * o_win).reshape(B, T, Q_A)
    win_keep = min(WINDOW, t0 + T)
    win_state = win_ext[:, win_ext.shape[1] - win_keep:]

    o_b, pool_state = pool_mixer(u_b, pool_prev, t0, p['pool_w'], p['pool_scale'])

    bg, cg, hc = jnp.split(cv, 3, axis=-1)
    z, conv_state = causal_dwconv(cg * hc, conv_prev, p['conv_w'])
    o_c = bg * z

    qd = q_d.reshape(B, T, DIFF_HEADS, 2, DIFF_QK)
    kd = kv[..., 4 * a:4 * a + Q_D].reshape(B, L, DIFF_HEADS, 2, DIFF_QK)
    vd = kv[..., 4 * a + Q_D:].reshape(B, L, DIFF_HEADS, DIFF_V)
    lam_init = 0.8 - 0.6 * math.exp(-0.3 * l)
    lp = p['diff_lam'].astype(jnp.float32)
    lam = jnp.exp(jnp.sum(lp[0] * lp[1])) - jnp.exp(jnp.sum(lp[2] * lp[3])) + lam_init
    o_d = diff_attention(qd, t_pos, kd, vd, lam, lam_init, p['diff_norm_g']).reshape(B, T, DIFF_HEADS * DIFF_V)

    branches = jnp.stack([o_a, o_b, o_c, o_d], axis=2)
    proj_b = jnp.einsum('btnc,ncd->btnd', branches, p['w_branch'])
    g = jax.nn.sigmoid(gates.reshape(B, T, N_BRANCH, D_MODEL))
    mixed = jnp.sum(g * proj_b, axis=2) @ p['w_out']
    x = x + gt1 * mixed

    h2 = rms_norm(x, p['norm_g'][1]) * (1 + sc2) + sh2
    up = h2 @ p['w_up']
    upc, ffn_state = causal_dwconv(up, ffn_prev, p['ffn_conv'])
    val, gg = jnp.split(upc, 2, axis=-1)
    x = x + gt2 * ((jax.nn.silu(gg) * val) @ p['w_down'])
    return x, (kv_new, win_state, pool_state, conv_state, ffn_state)


def setup_inputs(seed: int = 0) -> dict:
    key = jax.random.key(seed)
    k = jax.random.split(key, 32)
    n_pages = PAST_LEN // PAGE_SIZE
    n_used = DEC_BATCH * n_pages
    n_pool = n_used + max(1, n_used // 4)
    perm = jax.random.permutation(k[0], n_pool)
    page_table = perm[:n_used].reshape(DEC_BATCH, n_pages).astype(jnp.int32)
    win_keep = min(WINDOW, PAST_LEN)
    nrm = lambda kk, shape, s: jax.random.normal(kk, shape, jnp.float32) * s
    return {
        'x_prompt': nrm(k[1], (BATCH, SEQ, D_MODEL), 1.0),
        'x_sample': nrm(k[2], (DEC_BATCH, DEC_SEQ, D_MODEL), 1.0),
        'cache_kv': nrm(k[3], (DEPTH, n_pool, PAGE_SIZE, KV_W), 1.0),
        'state_win_kv': nrm(k[4], (DEPTH, DEC_BATCH, win_keep, WIN_W), 1.0),
        'state_pool': nrm(k[5], (DEPTH, DEC_BATCH, POOL_KEEP, POOL_W), 1.0),
        'state_conv': nrm(k[6], (DEPTH, DEC_BATCH, CONV_K - 1, CONV_W), 1.0),
        'state_ffn': nrm(k[7], (DEPTH, DEC_BATCH, FFN_K - 1, 2 * D_FF), 1.0),
        'page_table': page_table,
        'c_prompt': nrm(k[8], (BATCH, D_MODEL), 1.0),
        'c_sample': nrm(k[9], (DEC_BATCH, D_MODEL), 1.0),
        'norm_g': 1.0 + nrm(k[10], (DEPTH, 2, D_MODEL), 0.01),
        'ada_w': nrm(k[11], (DEPTH, D_MODEL, 6 * D_MODEL), 0.5 * D_MODEL ** -0.5),
        'ada_b': nrm(k[12], (DEPTH, 6 * D_MODEL), 0.01),
        'w_in': nrm(k[13], (DEPTH, D_MODEL, PROJ_W), D_MODEL ** -0.5),
        'cmp_pe': nrm(k[14], (DEPTH, 2, CMP_BLOCK, NSA_HD), 0.1),
        'cmp_w1': nrm(k[15], (DEPTH, 2, CMP_BLOCK * NSA_HD, CMP_HID), (CMP_BLOCK * NSA_HD) ** -0.5),
        'cmp_w2': nrm(k[16], (DEPTH, 2, CMP_HID, NSA_HD), CMP_HID ** -0.5),
        'diff_lam': nrm(k[17], (DEPTH, 4, DIFF_QK), 0.1),
        'diff_norm_g': 1.0 + nrm(k[18], (DEPTH, DIFF_V), 0.01),
        'pool_w': nrm(k[19], (DEPTH, len(POOL_WINDOWS), POOL_GROUP, POOL_GROUP), POOL_GROUP ** -0.5),
        'pool_scale': 1.0 + nrm(k[20], (DEPTH, POOL_W), 0.1),
        'conv_w': nrm(k[21], (DEPTH, CONV_K, CONV_W), CONV_K ** -0.5),
        'w_branch': nrm(k[22], (DEPTH, N_BRANCH, BRANCH_W, D_MODEL), BRANCH_W ** -0.5),
        'w_out': nrm(k[23], (DEPTH, D_MODEL, D_MODEL), D_MODEL ** -0.5),
        'w_up': nrm(k[24], (DEPTH, D_MODEL, 2 * D_FF), D_MODEL ** -0.5),
        'ffn_conv': nrm(k[25], (DEPTH, FFN_K, 2 * D_FF), FFN_K ** -0.5),
        'w_down': nrm(k[26], (DEPTH, D_FF, D_MODEL), D_FF ** -0.5),
        'final_g': 1.0 + nrm(k[27], (D_MODEL,), 0.01),
    }


def reference(x_prompt, x_sample, cache_kv, state_win_kv, state_pool, state_conv, state_ffn, page_table,
              c_prompt, c_sample, norm_g, ada_w, ada_b, w_in, cmp_pe, cmp_w1, cmp_w2, diff_lam, diff_norm_g,
              pool_w, pool_scale, conv_w, w_branch, w_out, w_up, ffn_conv, w_down, final_g):
    def layer_params(l):
        return {'norm_g': norm_g[l], 'ada_w': ada_w[l], 'ada_b': ada_b[l], 'w_in': w_in[l],
                'cmp_pe': cmp_pe[l], 'cmp_w1': cmp_w1[l], 'cmp_w2': cmp_w2[l], 'diff_lam': diff_lam[l],
                'diff_norm_g': diff_norm_g[l], 'pool_w': pool_w[l], 'pool_scale': pool_scale[l],
                'conv_w': conv_w[l], 'w_branch': w_branch[l], 'w_out': w_out[l], 'w_up': w_up[l],
                'ffn_conv': ffn_conv[l], 'w_down': w_down[l]}

    Bp = x_prompt.shape[0]
    dt = x_prompt.dtype
    win0 = jnp.zeros((Bp, WINDOW, WIN_W), dt)
    pool0 = jnp.zeros((Bp, POOL_KEEP, POOL_W), dt)
    conv0 = jnp.zeros((Bp, CONV_K - 1, CONV_W), dt)
    ffn0 = jnp.zeros((Bp, FFN_K - 1, 2 * D_FF), dt)
    x = x_prompt
    kv_p, win_p, pool_p, conv_p, ffn_p = [], [], [], [], []
    for l in range(DEPTH):
        x, st = trunk_layer(l, x, c_prompt, None, win0, pool0, conv0, ffn0, 0, layer_params(l))
        kv_p.append(st[0]); win_p.append(st[1]); pool_p.append(st[2]); conv_p.append(st[3]); ffn_p.append(st[4])
    y_prompt = rms_norm(x, final_g)

    Bs = x_sample.shape[0]
    x = x_sample
    kv_s, win_s, pool_s, conv_s, ffn_s = [], [], [], [], []
    for l in range(DEPTH):
        past = cache_kv[l][page_table].reshape(Bs, page_table.shape[1] * PAGE_SIZE, KV_W)
        x, st = trunk_layer(l, x, c_sample, past, state_win_kv[l], state_pool[l], state_conv[l],
                            state_ffn[l], PAST_LEN, layer_params(l))
        kv_s.append(st[0]); win_s.append(st[1]); pool_s.append(st[2]); conv_s.append(st[3]); ffn_s.append(st[4])
    y_sample = rms_norm(x, final_g)

    kv_rows_prompt = jnp.stack(kv_p)
    win_prompt = jnp.stack(win_p)
    pool_prompt = jnp.stack(pool_p)
    conv_prompt = jnp.stack(conv_p)
    ffn_prompt = jnp.stack(ffn_p)
    kv_rows_sample = jnp.stack(kv_s)
    win_sample = jnp.stack(win_s)
    pool_sample = jnp.stack(pool_s)
    conv_sample = jnp.stack(conv_s)
    ffn_sample = jnp.stack(ffn_s)
    return (y_prompt, y_sample, kv_rows_prompt, win_prompt, pool_prompt, conv_prompt, ffn_prompt,
            kv_rows_sample, win_sample, pool_sample, conv_sample, ffn_sample)
```

```python
import functools
import math

import numpy as np
import jax
import jax.numpy as jnp
from jax import lax
from jax.experimental import pallas as pl
from jax.experimental.pallas import tpu as pltpu

D_MODEL = 1024
NSA_HEADS = 4
NSA_HD = 64
CMP_STRIDE = 16
CMP_BLOCK = 32
CMP_HID = 128
SEL_BLOCK = 64
SEL_TOP_N = 16
WINDOW = 512
POOL_WINDOWS = (2, 4, 8, 16)
POOL_GROUP = 64
POOL_W = 256
POOL_KEEP = 15
CONV_W = 256
DIFF_HEADS = 4
DIFF_QK = 32
DIFF_V = 64
N_BRANCH = 4
BRANCH_W = 256
D_FF = 2816
QBLOCK = 128
PAGE_SIZE = 128
EPS = 1e-6
KV_W = 768
WIN_W = 128
GA_W = 12

P_KV, P_CV, P_QA, P_UB, P_QD, P_WIN, P_GA, P_W = 0, 768, 1536, 1792, 2048, 2304, 2432, 2560
KVB_W = KV_W + WIN_W

LANES = 128
SUBLANES = 8
VMEM_LIMIT = 56 * 1024 * 1024

NEG = -1e30
BIG = 1e30
BF = jnp.bfloat16
F32 = jnp.float32
HI = lax.Precision.HIGHEST

ROW_TILE = 512
KEY_CHUNK = 256
FF_CHUNK = 256
HALO = 16
FFN_HALO = 8

_NT = (((1,), (1,)), ((), ()))


def _dot(a, b, precision=None):
    return jnp.dot(a, b, preferred_element_type=F32, precision=precision)


def _dot_nt(a, b, precision=None):
    return lax.dot_general(a, b, _NT, preferred_element_type=F32, precision=precision)


def _slopes(n):
    return [float(2.0 ** (-8.0 * (k + 1) / n)) for k in range(n)]


def _norm_mod(x, g, scale, shift):
    ms = jnp.mean(x * x, axis=-1, keepdims=True)
    return (x * lax.rsqrt(ms + EPS) * g) * (1.0 + scale) + shift


def _cparams(sem):
    return pltpu.CompilerParams(dimension_semantics=sem, vmem_limit_bytes=VMEM_LIMIT)


def _const_spec(shape):
    nd = len(shape)
    return pl.BlockSpec(shape, lambda *a: (0,) * nd, pipeline_mode=pl.Buffered(1))


def _ada_kernel(c_ref, w_ref, b_ref, o_ref):
    c = c_ref[...]
    a = (c * jax.nn.sigmoid(c)).astype(BF)
    o_ref[0] = _dot(a, w_ref[0].astype(BF)) + b_ref[0]


def _ada_call(c_all, ada_w, ada_b):
    depth = ada_w.shape[0]
    nb = c_all.shape[0]
    return pl.pallas_call(
        _ada_kernel,
        out_shape=jax.ShapeDtypeStruct((depth, nb, 6 * D_MODEL), F32),
        grid=(depth, 6),
        in_specs=[
            pl.BlockSpec((nb, D_MODEL), lambda l, j: (0, 0)),
            pl.BlockSpec((1, D_MODEL, D_MODEL), lambda l, j: (l, 0, j)),
            pl.BlockSpec((1, 1, D_MODEL), lambda l, j: (l, 0, j)),
        ],
        out_specs=pl.BlockSpec((1, nb, D_MODEL), lambda l, j: (l, 0, j)),
        compiler_params=_cparams(("arbitrary", "arbitrary")),
    )(c_all, ada_w, ada_b.reshape(depth, 1, 6 * D_MODEL))


def _proj_kernel(x_ref, sc_ref, sh_ref, g_ref, w_ref, proj_ref, kvb_ref):
    h = _norm_mod(x_ref[0], g_ref[...], sc_ref[0, 0, 0], sh_ref[0, 0, 0]).astype(BF)
    y = _dot(h, w_ref[...])
    proj_ref[0] = y
    kvb_ref[0] = jnp.concatenate([y[:, P_KV:P_KV + KV_W], y[:, P_WIN:P_WIN + WIN_W]], axis=1).astype(BF)


def _mod_spec(l, which, rm):
    return pl.BlockSpec((1, 1, 1, rm, D_MODEL), lambda b, i: (l, b, which, 0, 0))


def _proj_call(l, x3, mod, g, w_packed, tm):
    bx, tx, _ = x3.shape
    rm = mod.shape[3]
    return pl.pallas_call(
        _proj_kernel,
        out_shape=(jax.ShapeDtypeStruct((bx, tx, P_W), F32), jax.ShapeDtypeStruct((bx, tx, KVB_W), BF)),
        grid=(bx, tx // tm),
        in_specs=[
            pl.BlockSpec((1, tm, D_MODEL), lambda b, i: (b, i, 0)),
            _mod_spec(l, 1, rm), _mod_spec(l, 0, rm),
            _const_spec((1, D_MODEL)),
            _const_spec((D_MODEL, P_W)),
        ],
        out_specs=(pl.BlockSpec((1, tm, P_W), lambda b, i: (b, i, 0)),
                   pl.BlockSpec((1, tm, KVB_W), lambda b, i: (b, i, 0))),
        compiler_params=_cparams(("arbitrary", "arbitrary")),
    )(x3, mod, mod, g, w_packed)


def _compress_finish(y, yb, w2):
    n = y.shape[0]
    yn = pltpu.roll(y, n - 1, axis=0)
    hid = CMP_HID
    hk = y[:, 0:hid] + yn[:, hid:2 * hid] + (yb[0:1, 0:hid] + yb[1:2, hid:2 * hid])
    hv = y[:, 2 * hid:3 * hid] + yn[:, 3 * hid:4 * hid] + (yb[0:1, 2 * hid:3 * hid] + yb[1:2, 3 * hid:4 * hid])
    act = jnp.concatenate([jax.nn.gelu(hk), jax.nn.gelu(hv)], axis=1).astype(BF)
    return _dot(act, w2)


def _compress_kernel(ch_ref, wc_ref, pe_ref, w2_ref, o_ref):
    y = _dot(ch_ref[0], wc_ref[...])
    yb = _dot(pe_ref[...], wc_ref[...])
    o_ref[0] = _compress_finish(y, yb, w2_ref[...]).astype(BF)


def _compress_call(chunks, wc, pe2, w2):
    b, n_ch, cw = chunks.shape
    return pl.pallas_call(
        _compress_kernel,
        out_shape=jax.ShapeDtypeStruct((b, n_ch, 2 * NSA_HD), BF),
        grid=(b,),
        in_specs=[pl.BlockSpec((1, n_ch, cw), lambda i: (i, 0, 0)),
                  _const_spec(wc.shape), _const_spec(pe2.shape), _const_spec(w2.shape)],
        out_specs=pl.BlockSpec((1, n_ch, 2 * NSA_HD), lambda i: (i, 0, 0)),
        compiler_params=_cparams(("arbitrary",)),
    )(chunks, wc, pe2, w2)


def _lam_value(lam_ref, lam_init):
    lp = lam_ref[0]
    a = jnp.sum(lp[0:1] * lp[1:2], axis=-1, keepdims=True)
    b = jnp.sum(lp[2:3] * lp[3:4], axis=-1, keepdims=True)
    return jnp.exp(a) - jnp.exp(b) + lam_init


def _group_rms(od, gmat, gain, lam_init):
    ms = _dot(od * od, gmat, precision=HI)
    return od * lax.rsqrt(ms + EPS) * gain * (1.0 - lam_init)


def _prompt_attn_kernel(qa_ref, qd_ref, ga_ref, kvb_ref, kc_ref, ovt_ref, e_ref, gmat_ref, lam_ref, gain_ref,
                        oa_ref, od_ref, m_s, l_s, acc_s, m_d, l_d, acc_d, *, lam_init, seq, top):
    tq = QBLOCK
    kc_len = KEY_CHUNK
    i = pl.program_id(1)
    t0 = i * tq
    n_sel = e_ref.shape[0]
    nsa_slopes = _slopes(NSA_HEADS)
    diff_slopes = _slopes(DIFF_HEADS)

    lane = lax.broadcasted_iota(jnp.int32, (1, LANES), 1)
    lo = lane < NSA_HD
    t_col = t0 + lax.broadcasted_iota(jnp.int32, (tq, 1), 0)
    t_row = t0 + lane

    qa = qa_ref[0] * (NSA_HD ** -0.5)
    qa_t = [qa[:, 0:LANES], qa[:, LANES:2 * LANES]]
    q_heads = []
    for h in range(NSA_HEADS):
        tile = qa_t[h // 2]
        if h % 2:
            tile = pltpu.roll(tile, NSA_HD, axis=1)
        q_heads.append(jnp.where(lo, tile, 0.0))
    q_nsa = jnp.concatenate(q_heads, axis=0).astype(BF)

    kc = kc_ref[0]
    n_ch = kc.shape[0]
    s = _dot_nt(q_nsa, kc)
    cmp_end = lax.broadcasted_iota(jnp.int32, (1, n_ch), 1) * CMP_STRIDE + (CMP_BLOCK - 1)
    vis = cmp_end <= t_col
    p_heads = []
    for h in range(NSA_HEADS):
        sh = jnp.where(vis, s[h * tq:(h + 1) * tq], NEG)
        m = jnp.max(sh, axis=-1, keepdims=True)
        e = jnp.where(vis, jnp.exp(sh - m), 0.0)
        d = jnp.sum(e, axis=-1, keepdims=True)
        p_heads.append(e / jnp.where(d > 0, d, 1.0))
    p_c = jnp.concatenate(p_heads, axis=0)
    o_cmp = _dot(p_c.astype(BF), kc)
    p_sum = p_heads[0] + p_heads[1] + p_heads[2] + p_heads[3]
    imp_t = _dot_nt(ovt_ref[...], p_sum, precision=HI)

    j_full = lax.broadcasted_iota(jnp.int32, (n_sel, LANES), 0)
    valid = j_full * SEL_BLOCK <= t_row
    cur = lax.shift_right_logical(t_row, 6)
    forced = jnp.where(j_full == 0, 1, jnp.where(j_full == cur, 1, jnp.where(j_full == cur - 1, 1, 0)))
    score = jnp.where(valid, jnp.where(forced > 0, BIG, imp_t), -BIG)
    rank = jnp.zeros((n_sel, LANES), F32)
    for ii in range(n_sel):
        si = score[ii:ii + 1, :]
        rank = rank + jnp.where(si > score, 1.0, jnp.where(si == score, jnp.where(j_full > ii, 1.0, 0.0), 0.0))
    sel_t = jnp.where(rank < top, 1.0, 0.0)
    sel = sel_t.T.astype(BF)

    qd = qd_ref[0] * (DIFF_QK ** -0.5)
    q_diff = []
    for g in range(2):
        tile = qd[:, g * LANES:(g + 1) * LANES]
        slabs = [jnp.where(lax.shift_right_logical(lane, 5) == sl, tile, 0.0) for sl in range(4)]
        q_diff.append(jnp.concatenate(slabs, axis=0).astype(BF))

    m_s[...] = jnp.full(m_s.shape, NEG, F32)
    l_s[...] = jnp.zeros(l_s.shape, F32)
    acc_s[...] = jnp.zeros(acc_s.shape, F32)
    m_d[...] = jnp.full(m_d.shape, NEG, F32)
    l_d[...] = jnp.zeros(l_d.shape, F32)
    acc_d[...] = jnp.zeros(acc_d.shape, F32)

    def online_update(sc_all, bias_rows, mask_bias, v_tile, m_ref, l_ref, acc_ref):
        es, alphas = [], []
        for sl in range(4):
            rows = slice(sl * tq, (sl + 1) * tq)
            s_h = sc_all[rows] + bias_rows[sl] + mask_bias
            m_old = m_ref[rows]
            m_new = jnp.maximum(m_old, jnp.max(s_h, axis=-1, keepdims=True))
            alpha = jnp.exp(m_old - m_new)
            e = jnp.exp(s_h - m_new)
            l_ref[rows] = alpha * l_ref[rows] + jnp.sum(e, axis=-1, keepdims=True)
            m_ref[rows] = m_new
            es.append(e.astype(BF))
            alphas.append(alpha)
        pv = _dot(jnp.concatenate(es, axis=0), v_tile)
        acc_ref[...] = jnp.concatenate(alphas, axis=0) * acc_ref[...] + pv

    n_chunks = (t0 + tq + kc_len - 1) // kc_len

    def chunk_body(c, carry):
        ks = pl.multiple_of(c * kc_len, kc_len)
        pos = ks + lax.broadcasted_iota(jnp.int32, (1, kc_len), 1)
        rel = (pos - t0).astype(F32)
        causal = pos <= t_col
        sel_exp = _dot(sel, e_ref[:, pl.ds(ks, kc_len)])
        mb_sel = jnp.where(causal, jnp.where(sel_exp > 0.5, 0.0, NEG), NEG)
        mb_d = jnp.where(causal, 0.0, NEG)
        kv_sel = kvb_ref[0, pl.ds(ks, kc_len), LANES:2 * LANES]
        sc = _dot_nt(q_nsa, kv_sel)
        online_update(sc, [nsa_slopes[h] * rel for h in range(NSA_HEADS)], mb_sel, kv_sel, m_s, l_s, acc_s)
        for g in range(2):
            k_d = kvb_ref[0, pl.ds(ks, kc_len), (2 + g) * LANES:(3 + g) * LANES]
            v_d = kvb_ref[0, pl.ds(ks, kc_len), (4 + g) * LANES:(5 + g) * LANES]
            sc = _dot_nt(q_diff[g], k_d)
            online_update(sc, [diff_slopes[2 * g + sl // 2] * rel for sl in range(4)], mb_d, v_d,
                          m_d.at[g], l_d.at[g], acc_d.at[g])
        return carry

    lax.fori_loop(0, n_chunks, chunk_body, 0)

    o_sel = acc_s[...] / l_s[...]

    span = WINDOW + tq
    start = pl.multiple_of(jnp.maximum(i - WINDOW // tq, 0) * tq, tq)
    kv_win = kvb_ref[0, pl.ds(start, span), KV_W:KV_W + WIN_W]
    pos_w = start + lax.broadcasted_iota(jnp.int32, (1, span), 1)
    dist = t_col - pos_w
    mb_w = jnp.where(dist >= 0, jnp.where(dist < WINDOW, 0.0, NEG), NEG)
    rel_w = (pos_w - t0).astype(F32)
    sw = _dot_nt(q_nsa, kv_win)
    pw = []
    for h in range(NSA_HEADS):
        s_h = sw[h * tq:(h + 1) * tq] + nsa_slopes[h] * rel_w + mb_w
        e = jnp.exp(s_h - jnp.max(s_h, axis=-1, keepdims=True))
        pw.append((e / jnp.sum(e, axis=-1, keepdims=True)).astype(BF))
    o_win = _dot(jnp.concatenate(pw, axis=0), kv_win)

    ga = jax.nn.sigmoid(ga_ref[0])
    oa_h = []
    for h in range(NSA_HEADS):
        rows = slice(h * tq, (h + 1) * tq)
        oa_h.append(ga[:, 3 * h:3 * h + 1] * o_cmp[rows] + ga[:, 3 * h + 1:3 * h + 2] * o_sel[rows]
                    + ga[:, 3 * h + 2:3 * h + 3] * o_win[rows])
    oa_ref[0] = jnp.concatenate(
        [jnp.where(lo, pltpu.roll(oa_h[0], NSA_HD, axis=1), oa_h[1]),
         jnp.where(lo, pltpu.roll(oa_h[2], NSA_HD, axis=1), oa_h[3])], axis=1)

    lam = _lam_value(lam_ref, lam_init)
    tiles = []
    for g in range(2):
        on = acc_d[g] / l_d[g]
        a0 = on[0:tq] - lam * on[tq:2 * tq]
        a1 = on[2 * tq:3 * tq] - lam * on[3 * tq:4 * tq]
        tiles.append(jnp.where(lo, a0, a1))
    od = jnp.concatenate(tiles, axis=1)
    od_ref[0] = _group_rms(od, gmat_ref[...], gain_ref[0], lam_init)


def _prompt_attn_call(l, proj, kvb, kcvc, ovt, emat, gmat, diff_lam, gain, lam_init):
    b, t, _ = proj.shape
    n_ch = kcvc.shape[1]
    n_sel = emat.shape[0]
    tq = QBLOCK
    kern = functools.partial(_prompt_attn_kernel, lam_init=lam_init, seq=t, top=min(SEL_TOP_N, n_sel))
    return pl.pallas_call(
        kern,
        out_shape=(jax.ShapeDtypeStruct((b, t, BRANCH_W), F32), jax.ShapeDtypeStruct((b, t, BRANCH_W), F32)),
        grid=(b, t // tq),
        in_specs=[
            pl.BlockSpec((1, tq, 256), lambda bi, i: (bi, i, P_QA // 256)),
            pl.BlockSpec((1, tq, 256), lambda bi, i: (bi, i, P_QD // 256)),
            pl.BlockSpec((1, tq, LANES), lambda bi, i: (bi, i, P_GA // LANES)),
            pl.BlockSpec((1, t, KVB_W), lambda bi, i: (bi, 0, 0)),
            pl.BlockSpec((1, n_ch, 2 * NSA_HD), lambda bi, i: (bi, 0, 0)),
            _const_spec(ovt.shape), _const_spec(emat.shape), _const_spec(gmat.shape),
            pl.BlockSpec((1, 4, DIFF_QK), lambda bi, i: (l, 0, 0)),
            pl.BlockSpec((1, 1, BRANCH_W), lambda bi, i: (l, 0, 0)),
        ],
        out_specs=(pl.BlockSpec((1, tq, BRANCH_W), lambda bi, i: (bi, i, 0)),
                   pl.BlockSpec((1, tq, BRANCH_W), lambda bi, i: (bi, i, 0))),
        scratch_shapes=[
            pltpu.VMEM((4 * tq, 1), F32), pltpu.VMEM((4 * tq, 1), F32), pltpu.VMEM((4 * tq, LANES), F32),
            pltpu.VMEM((2, 4 * tq, 1), F32), pltpu.VMEM((2, 4 * tq, 1), F32), pltpu.VMEM((2, 4 * tq, LANES), F32),
        ],
        compiler_params=_cparams(("arbitrary", "arbitrary")),
    )(proj, proj, proj, kvb, kcvc, ovt, emat, gmat, diff_lam, gain)


def _sample_attn_kernel(pt_ref, proj_ref, *refs, n_pages, lam_init, top, n_sel):
    del pt_ref
    pages = refs[:n_pages]
    (swin_ref, wc_ref, pe_ref, w2_ref, ov_ref, e_ref, gmat_ref, lam_ref, gain_ref, oa_ref, od_ref,
     cmp_rows) = refs[n_pages:]
    past = n_pages * PAGE_SIZE
    n_chp = past // CMP_STRIDE
    n_vis = (past - (CMP_BLOCK - 1)) // CMP_STRIDE + 1
    cur = past // SEL_BLOCK
    nsa_slopes = _slopes(NSA_HEADS)
    diff_slopes = _slopes(DIFF_HEADS)

    row = proj_ref[0]
    kv_new = row[:, P_KV:P_KV + KV_W]
    lane = lax.broadcasted_iota(jnp.int32, (1, LANES), 1)
    lo = lane < NSA_HD
    rid = lax.broadcasted_iota(jnp.int32, (SUBLANES, 1), 0)
    rid_full = lax.broadcasted_iota(jnp.int32, (SUBLANES, LANES), 0)

    def bcast(v):
        return jnp.broadcast_to(v, (SUBLANES, v.shape[1]))

    qa = row[:, P_QA:P_QA + 256] * (NSA_HD ** -0.5)
    b0, b1 = bcast(qa[:, 0:LANES]), bcast(qa[:, LANES:2 * LANES])
    cand = [b0, pltpu.roll(b0, NSA_HD, axis=1), b1, pltpu.roll(b1, NSA_HD, axis=1)]
    q8 = jnp.zeros((SUBLANES, LANES), F32)
    for h in range(NSA_HEADS):
        q8 = jnp.where(rid_full == h, cand[h], q8)
    q8 = jnp.where(lo, q8, 0.0)
    q8b = q8.astype(BF)
    slope_col = jnp.zeros((SUBLANES, 1), F32)
    for h in range(NSA_HEADS):
        slope_col = jnp.where(rid == h, nsa_slopes[h], slope_col)

    for j, pg in enumerate(pages):
        cmp_rows[j * PAGE_SIZE:(j + 1) * PAGE_SIZE, :] = pg[0, 0, :, 0:LANES]
    y = jnp.zeros((n_chp, 4 * CMP_HID), F32)
    for r in range(CMP_STRIDE):
        xr = cmp_rows[pl.ds(r, n_chp, stride=CMP_STRIDE), :].astype(BF)
        y = y + _dot(xr, wc_ref[r * LANES:(r + 1) * LANES, :])
    yb = _dot(pe_ref[...], wc_ref[...])
    kc = _compress_finish(y, yb, w2_ref[...]).astype(BF)

    cl = lax.broadcasted_iota(jnp.int32, (1, n_chp), 1)
    vis = cl < n_vis
    s = jnp.where(vis, _dot_nt(q8b, kc), NEG)
    m = jnp.max(s, axis=-1, keepdims=True)
    e = jnp.where(vis, jnp.exp(s - m), 0.0)
    d = jnp.sum(e, axis=-1, keepdims=True)
    p_c = jnp.where(rid < NSA_HEADS, e / jnp.where(d > 0, d, 1.0), 0.0)
    o_cmp = _dot(p_c.astype(BF), kc)
    imp = jnp.sum(_dot(p_c, ov_ref[...], precision=HI), axis=0, keepdims=True)

    forced = jnp.where(lane == 0, 1, jnp.where(lane == cur, 1, jnp.where(lane == cur - 1, 1, 0)))
    score = jnp.where(lane < n_sel, jnp.where(forced > 0, BIG, imp), -BIG)
    r_row = jnp.broadcast_to(score, (LANES, LANES))
    r_col = r_row.T
    ii = lax.broadcasted_iota(jnp.int32, (LANES, LANES), 0)
    jj = lax.broadcasted_iota(jnp.int32, (LANES, LANES), 1)
    inc = jnp.where(r_col > r_row, 1.0, jnp.where(r_col == r_row, jnp.where(ii < jj, 1.0, 0.0), 0.0))
    rank = jnp.sum(inc, axis=0, keepdims=True)
    sel = jnp.where(rank < top, 1.0, 0.0)
    sel_exp = _dot(bcast(sel).astype(BF), e_ref[...])

    pos = lax.broadcasted_iota(jnp.int32, (1, past), 1)
    rel = (pos - past).astype(F32)

    def attend_new(q, sc, k_tile, kv_tile_new):
        s_new = jnp.sum(q * kv_tile_new, axis=-1, keepdims=True)
        mm = jnp.maximum(jnp.max(sc, axis=-1, keepdims=True), s_new)
        ee = jnp.exp(sc - mm)
        e_new = jnp.exp(s_new - mm)
        den = jnp.sum(ee, axis=-1, keepdims=True) + e_new
        return (_dot(ee.astype(BF), k_tile) + e_new * kv_tile_new) / den

    k_sel = jnp.concatenate([pg[0, 0, :, LANES:2 * LANES] for pg in pages], axis=0).astype(BF)
    sc = _dot_nt(q8b, k_sel) + slope_col * rel + jnp.where(sel_exp > 0.5, 0.0, NEG)
    o_sel = attend_new(q8, sc, k_sel, kv_new[:, LANES:2 * LANES])

    k_win = swin_ref[0, 0].astype(BF)
    nw = k_win.shape[0]
    wl = lax.broadcasted_iota(jnp.int32, (1, nw), 1)
    dist_w = nw - wl
    sc = _dot_nt(q8b, k_win) - slope_col * dist_w.astype(F32) + jnp.where(dist_w < WINDOW, 0.0, NEG)
    o_win = attend_new(q8, sc, k_win, row[:, P_WIN:P_WIN + WIN_W])

    ga8 = bcast(jax.nn.sigmoid(row[:, P_GA:P_GA + LANES]))
    gates = [jnp.sum(jnp.where(lane == 3 * rid_full + k, ga8, 0.0), axis=-1, keepdims=True) for k in range(3)]
    oa8 = gates[0] * o_cmp + gates[1] * o_sel + gates[2] * o_win
    oa_ref[0] = jnp.concatenate(
        [jnp.where(lo, pltpu.roll(oa8[0:1], NSA_HD, axis=1), oa8[1:2]),
         jnp.where(lo, pltpu.roll(oa8[2:3], NSA_HD, axis=1), oa8[3:4])], axis=1)

    qd = bcast(row[:, P_QD:P_QD + 256] * (DIFF_QK ** -0.5))
    lane2 = lax.broadcasted_iota(jnp.int32, (SUBLANES, 2 * LANES), 1)
    rid2 = lax.broadcasted_iota(jnp.int32, (SUBLANES, 2 * LANES), 0)
    qd8 = jnp.where(lax.shift_right_logical(lane2, 5) == rid2, qd, 0.0)
    dslope = jnp.zeros((SUBLANES, 1), F32)
    for r in range(SUBLANES):
        dslope = jnp.where(rid == r, diff_slopes[r // 2], dslope)
    k_d = jnp.concatenate([pg[0, 0, :, 2 * LANES:4 * LANES] for pg in pages], axis=0).astype(BF)
    v_d = jnp.concatenate([pg[0, 0, :, 4 * LANES:6 * LANES] for pg in pages], axis=0).astype(BF)
    sc = _dot_nt(qd8.astype(BF), k_d) + dslope * rel
    s_new = jnp.sum(qd8 * kv_new[:, 2 * LANES:4 * LANES], axis=-1, keepdims=True)
    mm = jnp.maximum(jnp.max(sc, axis=-1, keepdims=True), s_new)
    ee = jnp.exp(sc - mm)
    e_new = jnp.exp(s_new - mm)
    den = jnp.sum(ee, axis=-1, keepdims=True) + e_new
    o8 = (_dot(ee.astype(BF), v_d) + e_new * kv_new[:, 4 * LANES:6 * LANES]) / den
    lam = _lam_value(lam_ref, lam_init)
    coef = jnp.where((rid2 & 1) == 0, 1.0, -lam)
    head_mask = lax.shift_right_logical(lane2, 6) == lax.shift_right_logical(rid2, 1)
    od = jnp.sum(jnp.where(head_mask, coef * o8, 0.0), axis=0, keepdims=True)
    od8 = _group_rms(bcast(od), gmat_ref[...], gain_ref[0], lam_init)
    od_ref[0] = od8[0:1]


def _sample_attn_call(l, page_table, proj, cache_kv, state_win, wc, pe2, w2, ov, emat, gmat, diff_lam, gain,
                      lam_init, n_sel):
    bs = proj.shape[0]
    n_pages = page_table.shape[1]
    nw = state_win.shape[2]
    kern = functools.partial(_sample_attn_kernel, n_pages=n_pages, lam_init=lam_init,
                             top=min(SEL_TOP_N, n_sel), n_sel=n_sel)

    def page_spec(j):
        return pl.BlockSpec((1, 1, PAGE_SIZE, KV_W), lambda b, pt: (l, pt[b, j], 0, 0))

    def cspec(shape):
        nd = len(shape)
        return pl.BlockSpec(shape, lambda b, pt: (0,) * nd, pipeline_mode=pl.Buffered(1))

    grid_spec = pltpu.PrefetchScalarGridSpec(
        num_scalar_prefetch=1,
        grid=(bs,),
        in_specs=[pl.BlockSpec((1, 1, P_W), lambda b, pt: (b, 0, 0))]
        + [page_spec(j) for j in range(n_pages)]
        + [pl.BlockSpec((1, 1, nw, WIN_W), lambda b, pt: (l, b, 0, 0)),
           cspec(wc.shape), cspec(pe2.shape), cspec(w2.shape), cspec(ov.shape), cspec(emat.shape),
           cspec(gmat.shape),
           pl.BlockSpec((1, 4, DIFF_QK), lambda b, pt: (l, 0, 0)),
           pl.BlockSpec((1, 1, BRANCH_W), lambda b, pt: (l, 0, 0))],
        out_specs=(pl.BlockSpec((1, 1, BRANCH_W), lambda b, pt: (b, 0, 0)),
                   pl.BlockSpec((1, 1, BRANCH_W), lambda b, pt: (b, 0, 0))),
        scratch_shapes=[pltpu.VMEM((n_pages * PAGE_SIZE, LANES), F32)],
    )
    return pl.pallas_call(
        kern,
        out_shape=(jax.ShapeDtypeStruct((bs, 1, BRANCH_W), F32), jax.ShapeDtypeStruct((bs, 1, BRANCH_W), F32)),
        grid_spec=grid_spec,
        compiler_params=_cparams(("arbitrary",)),
    )(page_table, proj, *([cache_kv] * n_pages), state_win, wc, pe2, w2, ov, emat, gmat, diff_lam, gain)


def _merge_kernel(*refs, halo):
    if halo:
        (x_ref, sc_ref, sh_ref, gt_ref, g_ref, cv_ref, ub_ref, cvp_ref, ubp_ref, oa_ref, od_ref,
         wg_ref, wb_ref, wo_ref, pw_ref, ps_ref, cw_ref, xo_ref, zin_ref) = refs
    else:
        (x_ref, sc_ref, sh_ref, gt_ref, g_ref, cv_ref, ub_ref, stc_ref, stp_ref, oa_ref, od_ref,
         wg_ref, wb_ref, wo_ref, pw_ref, ps_ref, cw_ref, xo_ref, zin_ref) = refs
    i = pl.program_id(1)
    x = x_ref[0]
    tm = x.shape[0]
    h = _norm_mod(x, g_ref[...], sc_ref[0, 0, 0], sh_ref[0, 0, 0]).astype(BF)

    cv = cv_ref[0]
    u = ub_ref[0]
    bg = cv[:, 0:CONV_W]
    zin = cv[:, CONV_W:2 * CONV_W] * cv[:, 2 * CONV_W:3 * CONV_W]
    zin_ref[0] = zin
    cw = cw_ref[...]
    lane = lax.broadcasted_iota(jnp.int32, (1, POOL_W), 1)
    win_lane = jnp.where(lane < 64, 2, jnp.where(lane < 128, 4, jnp.where(lane < 192, 8, 16)))

    if halo:
        keep = jnp.where(i > 0, 1.0, 0.0)
        cvp = cvp_ref[0] * keep
        ubp = ubp_ref[0] * keep
        z_ext = jnp.concatenate([cvp[:, CONV_W:2 * CONV_W] * cvp[:, 2 * CONV_W:3 * CONV_W], zin], axis=0)
        z = (cw[0:1] * pltpu.roll(z_ext, 2, axis=0) + cw[1:2] * pltpu.roll(z_ext, 1, axis=0)
             + cw[2:3] * z_ext)[HALO:]
        ext = jnp.concatenate([ubp, u], axis=0)
        s2 = ext + pltpu.roll(ext, 1, axis=0)
        s4 = s2 + pltpu.roll(s2, 2, axis=0)
        s8 = s4 + pltpu.roll(s4, 4, axis=0)
        s16 = s8 + pltpu.roll(s8, 8, axis=0)
        tot = jnp.where(lane < 64, s2, jnp.where(lane < 128, s4, jnp.where(lane < 192, s8, s16)))[HALO:]
        pos1 = i * tm + lax.broadcasted_iota(jnp.int32, (tm, 1), 0) + 1
        cnt = jnp.minimum(win_lane, pos1).astype(F32)
    else:
        z = cw[0:1] * stc_ref[0] + cw[1:2] * stc_ref[1] + cw[2:3] * zin
        s2 = u + stp_ref[POOL_KEEP - 1]
        s4 = s2 + stp_ref[POOL_KEEP - 2] + stp_ref[POOL_KEEP - 3]
        s8 = s4
        for k in range(4, 8):
            s8 = s8 + stp_ref[POOL_KEEP - k]
        s16 = s8
        for k in range(8, 16):
            s16 = s16 + stp_ref[POOL_KEEP - k]
        tot = jnp.where(lane < 64, s2, jnp.where(lane < 128, s4, jnp.where(lane < 192, s8, s16)))
        cnt = win_lane.astype(F32)
    o_c = bg * z
    pooled = tot / cnt - u
    o_b = _dot(pooled.astype(BF), pw_ref[...]) * ps_ref[...]

    branches = [oa_ref[0], o_b, o_c, od_ref[0]]
    mixed = jnp.zeros((tm, D_MODEL), F32)
    for n in range(N_BRANCH):
        pb = _dot(branches[n].astype(BF), wb_ref[n])
        gate = jax.nn.sigmoid(_dot(h, wg_ref[:, n * D_MODEL:(n + 1) * D_MODEL]))
        mixed = mixed + gate * pb
    xo_ref[0] = x + gt_ref[0, 0, 0] * _dot(mixed.astype(BF), wo_ref[...])


def _merge_call(l, x3, mod, g, proj, prev_c, prev_p, o_a, o_d, wg, wb, wo, pw, ps, cw, tm, halo):
    bx, tx, _ = x3.shape
    rm = mod.shape[3]
    if halo:
        nblk = tm // HALO
        prev_specs = [
            pl.BlockSpec((1, HALO, 3 * CONV_W), lambda b, i: (b, jnp.maximum(i * nblk - 1, 0), P_CV // (3 * CONV_W))),
            pl.BlockSpec((1, HALO, POOL_W), lambda b, i: (b, jnp.maximum(i * nblk - 1, 0), P_UB // POOL_W)),
        ]
    else:
        prev_specs = [pl.BlockSpec(prev_c.shape, lambda b, i: (0, 0, 0)),
                      pl.BlockSpec(prev_p.shape, lambda b, i: (0, 0, 0))]
    return pl.pallas_call(
        functools.partial(_merge_kernel, halo=halo),
        out_shape=(jax.ShapeDtypeStruct((bx, tx, D_MODEL), F32), jax.ShapeDtypeStruct((bx, tx, CONV_W), F32)),
        grid=(bx, tx // tm),
        in_specs=[
            pl.BlockSpec((1, tm, D_MODEL), lambda b, i: (b, i, 0)),
            _mod_spec(l, 1, rm), _mod_spec(l, 0, rm), _mod_spec(l, 2, rm),
            _const_spec((1, D_MODEL)),
            pl.BlockSpec((1, tm, 3 * CONV_W), lambda b, i: (b, i, P_CV // (3 * CONV_W))),
            pl.BlockSpec((1, tm, POOL_W), lambda b, i: (b, i, P_UB // POOL_W)),
            *prev_specs,
            pl.BlockSpec((1, tm, BRANCH_W), lambda b, i: (b, i, 0)),
            pl.BlockSpec((1, tm, BRANCH_W), lambda b, i: (b, i, 0)),
            _const_spec(wg.shape), _const_spec(wb.shape), _const_spec(wo.shape),
            _const_spec(pw.shape), _const_spec(ps.shape), _const_spec(cw.shape),
        ],
        out_specs=(pl.BlockSpec((1, tm, D_MODEL), lambda b, i: (b, i, 0)),
                   pl.BlockSpec((1, tm, CONV_W), lambda b, i: (b, i, 0))),
        compiler_params=_cparams(("arbitrary", "arbitrary")),
    )(x3, mod, mod, mod, g, proj, proj, prev_c, prev_p, o_a, o_d, wg, wb, wo, pw, ps, cw)


def _ffn_kernel(*refs, halo, final, keep_rows):
    (x_ref, sc_ref, sh_ref, gt_ref, g_ref, prev_ref, wu_ref, fc_ref, wd_ref, fg_ref, xo_ref, up_ref) = refs
    i = pl.program_id(1)
    x = x_ref[0]
    tm = x.shape[0]
    g = g_ref[...]
    sc = sc_ref[0, 0, 0]
    sh = sh_ref[0, 0, 0]
    if halo:
        hx = jnp.concatenate([prev_ref[0], x], axis=0)
        h2 = _norm_mod(hx, g, sc, sh).astype(BF)
        rows = lax.broadcasted_iota(jnp.int32, (tm + FFN_HALO, 1), 0)
        live = jnp.where(rows >= FFN_HALO, 1.0, jnp.where(i > 0, 1.0, 0.0))
    else:
        h2 = _norm_mod(x, g, sc, sh).astype(BF)
    acc = jnp.zeros((tm, D_MODEL), F32)
    for c in range(D_FF // FF_CHUNK):
        halves = []
        for part in range(2):
            col = part * D_FF + c * FF_CHUNK
            up = _dot(h2, wu_ref[:, col:col + FF_CHUNK])
            w = fc_ref[:, col:col + FF_CHUNK]
            if halo:
                up = up * live
                up_ref[0, :, col:col + FF_CHUNK] = up[FFN_HALO + tm - keep_rows:]
                upc = (w[0:1] * pltpu.roll(up, 2, axis=0) + w[1:2] * pltpu.roll(up, 1, axis=0)
                       + w[2:3] * up)[FFN_HALO:]
            else:
                up_ref[0, :, col:col + FF_CHUNK] = up
                upc = (w[0:1] * prev_ref[0, :, col:col + FF_CHUNK] + w[1:2] * prev_ref[1, :, col:col + FF_CHUNK]
                       + w[2:3] * up)
            halves.append(upc)
        val, gg = halves
        act = (gg * jax.nn.sigmoid(gg) * val).astype(BF)
        acc = acc + _dot(act, wd_ref[c * FF_CHUNK:(c + 1) * FF_CHUNK, :])
    out = x + gt_ref[0, 0, 0] * acc
    if final:
        ms = jnp.mean(out * out, axis=-1, keepdims=True)
        out = out * lax.rsqrt(ms + EPS) * fg_ref[...]
    xo_ref[0] = out


def _ffn_call(l, x3, mod, g, prev, wu, fc, wd, fg, tm, halo, final):
    bx, tx, _ = x3.shape
    rm = mod.shape[3]
    keep_rows = SUBLANES if halo else tm
    if halo:
        nblk = tm // FFN_HALO
        prev_spec = pl.BlockSpec((1, FFN_HALO, D_MODEL), lambda b, i: (b, jnp.maximum(i * nblk - 1, 0), 0))
    else:
        prev_spec = pl.BlockSpec(prev.shape, lambda b, i: (0, 0, 0))
    return pl.pallas_call(
        functools.partial(_ffn_kernel, halo=halo, final=final, keep_rows=keep_rows),
        out_shape=(jax.ShapeDtypeStruct((bx, tx, D_MODEL), F32),
                   jax.ShapeDtypeStruct((bx, keep_rows, 2 * D_FF), F32)),
        grid=(bx, tx // tm),
        in_specs=[
            pl.BlockSpec((1, tm, D_MODEL), lambda b, i: (b, i, 0)),
            _mod_spec(l, 4, rm), _mod_spec(l, 3, rm), _mod_spec(l, 5, rm),
            _const_spec((1, D_MODEL)),
            prev_spec,
            _const_spec(wu.shape), _const_spec(fc.shape), _const_spec(wd.shape), _const_spec((1, D_MODEL)),
        ],
        out_specs=(pl.BlockSpec((1, tm, D_MODEL), lambda b, i: (b, i, 0)),
                   pl.BlockSpec((1, keep_rows, 2 * D_FF), lambda b, i: (b, 0, 0))),
        compiler_params=_cparams(("arbitrary", "arbitrary")),
    )(x3, mod, mod, mod, g, prev, wu, fc, wd, fg)


def _pack_w_in(w_in):
    o = np.cumsum([0, 256, KV_W, WIN_W, GA_W, POOL_W, 3 * CONV_W, 256, N_BRANCH * D_MODEL])
    q_a, kv, win, g_a, u_b, cv, q_d, gates = [w_in[..., o[k]:o[k + 1]] for k in range(8)]
    g_a = jnp.pad(g_a, ((0, 0), (0, 0), (0, LANES - GA_W)))
    packed = jnp.concatenate([kv, cv, q_a, u_b, q_d, win, g_a], axis=-1).astype(BF)
    return packed, gates.astype(BF)


def _compress_weights(cmp_pe, cmp_w1, cmp_w2):
    depth = cmp_w1.shape[0]
    half = CMP_STRIDE * NSA_HD
    w1 = cmp_w1.reshape(depth, 2, 2, CMP_STRIDE, NSA_HD, CMP_HID)
    z = jnp.zeros((depth, CMP_STRIDE, NSA_HD, CMP_HID), cmp_w1.dtype)
    cols = []
    for kv in range(2):
        for hf in range(2):
            blk = w1[:, kv, hf]
            pair = (blk, z) if kv == 0 else (z, blk)
            cols.append(jnp.concatenate(pair, axis=2).reshape(depth, 2 * half, CMP_HID))
    wc = jnp.concatenate(cols, axis=-1).astype(BF)
    pe = cmp_pe.reshape(depth, 2, 2, CMP_STRIDE, NSA_HD)
    pe2 = jnp.concatenate([pe[:, 0], pe[:, 1]], axis=-1).reshape(depth, 2, 2 * half)
    pe2 = jnp.pad(pe2, ((0, 0), (0, SUBLANES - 2), (0, 0))).astype(BF)
    zz = jnp.zeros((depth, CMP_HID, NSA_HD), cmp_w2.dtype)
    w2 = jnp.concatenate([jnp.concatenate([cmp_w2[:, 0], zz], axis=-1),
                          jnp.concatenate([zz, cmp_w2[:, 1]], axis=-1)], axis=1).astype(BF)
    return wc, pe2, w2


def _overlap(n_cmp, n_sel):
    cs = np.arange(n_cmp)[:, None] * CMP_STRIDE
    ss = np.arange(n_sel)[None, :] * SEL_BLOCK
    ov = np.minimum(cs + CMP_BLOCK, ss + SEL_BLOCK) - np.maximum(cs, ss)
    return (np.maximum(ov, 0) / CMP_BLOCK).astype(np.float32)


def _expand_matrix(n_rows, n_pos):
    return (np.arange(n_pos)[None, :] // SEL_BLOCK == np.arange(n_rows)[:, None]).astype(np.float32)


def _group_mean_matrix():
    idx = np.arange(DIFF_HEADS * DIFF_V) // DIFF_V
    return (idx[:, None] == idx[None, :]).astype(np.float32) / DIFF_V


def _pool_blockdiag(pool_w):
    depth, n, g, _ = pool_w.shape
    out = jnp.zeros((depth, n * g, n * g), pool_w.dtype)
    for k in range(n):
        out = out.at[:, k * g:(k + 1) * g, k * g:(k + 1) * g].set(pool_w[:, k])
    return out.astype(BF)


def kernel(x_prompt, x_sample, cache_kv, state_win_kv, state_pool, state_conv, state_ffn, page_table, c_prompt, c_sample, norm_g, ada_w, ada_b, w_in, cmp_pe, cmp_w1, cmp_w2, diff_lam, diff_norm_g, pool_w, pool_scale, conv_w, w_branch, w_out, w_up, ffn_conv, w_down, final_g):
    depth = w_in.shape[0]
    bp, seq, _ = x_prompt.shape
    bs = x_sample.shape[0]
    n_pages = page_table.shape[1]
    past = n_pages * PAGE_SIZE
    tm = min(ROW_TILE, seq)
    assert seq % tm == 0 and seq % QBLOCK == 0 and seq >= WINDOW + QBLOCK and x_sample.shape[1] == 1

    w_packed, w_gate = _pack_w_in(w_in)
    wc, pe2, w2 = _compress_weights(cmp_pe, cmp_w1, cmp_w2)
    wb = w_branch.astype(BF)
    wo = w_out.astype(BF)
    wu = w_up.astype(BF)
    wd = w_down.astype(BF)
    pw = _pool_blockdiag(pool_w)
    ps = pool_scale.reshape(depth, 1, POOL_W)
    gain = jnp.tile(diff_norm_g, (1, DIFF_HEADS)).reshape(depth, 1, DIFF_HEADS * DIFF_V)
    fg = final_g.reshape(1, D_MODEL)
    gmat = jnp.asarray(_group_mean_matrix())

    n_ch = seq // CMP_STRIDE
    n_sel_p = seq // SEL_BLOCK
    ovt = np.zeros((n_sel_p, n_ch), np.float32)
    ovt[:, :n_ch - 1] = _overlap(n_ch - 1, n_sel_p).T
    ovt = jnp.asarray(ovt)
    emat_p = jnp.asarray(_expand_matrix(n_sel_p, seq), dtype=BF)
    n_sel_s = -(-(past + 1) // SEL_BLOCK)
    n_chp = past // CMP_STRIDE
    ov_s = np.zeros((n_chp, LANES), np.float32)
    ov_s[:, :n_sel_s] = _overlap(n_chp, n_sel_s)
    ov_s = jnp.asarray(ov_s)
    emat_s = jnp.asarray(_expand_matrix(LANES, past), dtype=BF)

    mod = _ada_call(jnp.concatenate([c_prompt, c_sample], axis=0), ada_w, ada_b)
    mod_p = mod[:, :bp].reshape(depth, bp, 6, 1, D_MODEL)
    mod_s = jnp.transpose(mod[:, bp:].reshape(depth, bs, 6, D_MODEL), (0, 2, 1, 3)).reshape(depth, 1, 6, bs, D_MODEL)

    x = x_prompt
    kv_p, win_p, pool_p, conv_p, ffn_p = [], [], [], [], []
    for l in range(depth):
        lam_init = 0.8 - 0.6 * math.exp(-0.3 * l)
        g1 = norm_g[l, 0].reshape(1, D_MODEL)
        g2 = norm_g[l, 1].reshape(1, D_MODEL)
        proj, kvb = _proj_call(l, x, mod_p, g1, w_packed[l], tm)
        chunks = kvb[:, :, 0:2 * NSA_HD].reshape(bp, n_ch, CMP_STRIDE * 2 * NSA_HD)
        kcvc = _compress_call(chunks, wc[l], pe2[l], w2[l])
        o_a, o_d = _prompt_attn_call(l, proj, kvb, kcvc, ovt, emat_p, gmat, diff_lam, gain, lam_init)
        x, zin = _merge_call(l, x, mod_p, g1, proj, proj, proj, o_a, o_d, w_gate[l], wb[l], wo[l], pw[l], ps[l],
                             conv_w[l], tm, True)
        x, up_last = _ffn_call(l, x, mod_p, g2, x, wu[l], ffn_conv[l], wd[l], fg, tm, True, l == depth - 1)
        kv_p.append(proj[:, :, P_KV:P_KV + KV_W])
        win_p.append(proj[:, seq - min(WINDOW, seq):, P_WIN:P_WIN + WIN_W])
        pool_p.append(proj[:, seq - POOL_KEEP:, P_UB:P_UB + POOL_W])
        conv_p.append(zin[:, seq - 2:])
        ffn_p.append(up_last[:, SUBLANES - 2:])
    y_prompt = x

    x = x_sample.reshape(1, bs, D_MODEL)
    kv_s, win_s, pool_s, conv_s, ffn_s = [], [], [], [], []
    for l in range(depth):
        lam_init = 0.8 - 0.6 * math.exp(-0.3 * l)
        g1 = norm_g[l, 0].reshape(1, D_MODEL)
        g2 = norm_g[l, 1].reshape(1, D_MODEL)
        proj, _ = _proj_call(l, x, mod_s, g1, w_packed[l], bs)
        proj_rows = proj.reshape(bs, 1, P_W)
        o_a, o_d = _sample_attn_call(l, page_table, proj_rows, cache_kv, state_win_kv, wc[l], pe2[l], w2[l], ov_s,
                                     emat_s, gmat, diff_lam, gain, lam_init, n_sel_s)
        st_conv = jnp.transpose(state_conv[l], (1, 0, 2))
        st_pool = jnp.transpose(state_pool[l], (1, 0, 2))
        st_ffn = jnp.transpose(state_ffn[l], (1, 0, 2))
        x, zin = _merge_call(l, x, mod_s, g1, proj, st_conv, st_pool, o_a.reshape(1, bs, BRANCH_W),
                             o_d.reshape(1, bs, BRANCH_W), w_gate[l], wb[l], wo[l], pw[l], ps[l], conv_w[l], bs, False)
        x, up_new = _ffn_call(l, x, mod_s, g2, st_ffn, wu[l], ffn_conv[l], wd[l], fg, bs, False, l == depth - 1)
        kv_s.append(proj[0, :, P_KV:P_KV + KV_W].reshape(bs, 1, KV_W))
        win_new = proj[0, :, P_WIN:P_WIN + WIN_W].reshape(bs, 1, WIN_W)
        win_ext = jnp.concatenate([state_win_kv[l], win_new], axis=1)
        win_s.append(win_ext[:, win_ext.shape[1] - min(WINDOW, past + 1):])
        u_new = proj[0, :, P_UB:P_UB + POOL_W].reshape(bs, 1, POOL_W)
        pool_s.append(jnp.concatenate([state_pool[l], u_new], axis=1)[:, 1:])
        conv_s.append(jnp.concatenate([state_conv[l], zin.reshape(bs, 1, CONV_W)], axis=1)[:, 1:])
        ffn_s.append(jnp.concatenate([state_ffn[l], up_new.reshape(bs, 1, 2 * D_FF)], axis=1)[:, 1:])
    y_sample = x.reshape(bs, 1, D_MODEL)

    return (y_prompt, y_sample, jnp.stack(kv_p), jnp.stack(win_p), jnp.stack(pool_p), jnp.stack(conv_p),
            jnp.stack(ffn_p), jnp.stack(kv_s), jnp.stack(win_s), jnp.stack(pool_s), jnp.stack(conv_s),
            jnp.stack(ffn_s))
```

```python
import functools
import math

import numpy as np
import jax
import jax.numpy as jnp
from jax import lax
from jax.experimental import pallas as pl
from jax.experimental.pallas import tpu as pltpu

D_MODEL = 1024
NSA_HEADS = 4
NSA_HD = 64
CMP_STRIDE = 16
CMP_BLOCK = 32
CMP_HID = 128
SEL_BLOCK = 64
SEL_TOP_N = 16
WINDOW = 512
POOL_WINDOWS = (2, 4, 8, 16)
POOL_GROUP = 64
POOL_W = 256
POOL_KEEP = 15
CONV_W = 256
DIFF_HEADS = 4
DIFF_QK = 32
DIFF_V = 64
N_BRANCH = 4
BRANCH_W = 256
D_FF = 2816
QBLOCK = 128
PAGE_SIZE = 128
EPS = 1e-6
KV_W = 768
WIN_W = 128
GA_W = 12

P_KV, P_CV, P_QA, P_UB, P_QD, P_WIN, P_GA, P_W = 0, 768, 1536, 1792, 2048, 2304, 2432, 2560
KVB_W = KV_W + WIN_W

LANES = 128
SUBLANES = 8
VMEM_LIMIT = 56 * 1024 * 1024

NEG = -1e30
BIG = 1e30
LOG2E = math.log2(math.e)
BF = jnp.bfloat16
F32 = jnp.float32
HI = lax.Precision.HIGHEST

ROW_TILE = 512
KEY_CHUNK = 256
FF_CHUNK = 256
HALO = 16
FFN_HALO = 8

_NT = (((1,), (1,)), ((), ()))


def _dot(a, b, precision=None):
    return jnp.dot(a, b, preferred_element_type=F32, precision=precision)


def _dot_nt(a, b, precision=None):
    return lax.dot_general(a, b, _NT, preferred_element_type=F32, precision=precision)


def _slopes(n):
    return [float(2.0 ** (-8.0 * (k + 1) / n)) for k in range(n)]


def _norm_mod(x, g, scale, shift):
    ms = jnp.mean(x * x, axis=-1, keepdims=True)
    return (x * lax.rsqrt(ms + EPS) * g) * (1.0 + scale) + shift


def _cparams(sem):
    return pltpu.CompilerParams(dimension_semantics=sem, vmem_limit_bytes=VMEM_LIMIT)


def _const_spec(shape):
    nd = len(shape)
    return pl.BlockSpec(shape, lambda *a: (0,) * nd, pipeline_mode=pl.Buffered(1))


def _ada_kernel(c_ref, w_ref, b_ref, o_ref):
    c = c_ref[...]
    a = (c * jax.nn.sigmoid(c)).astype(BF)
    o_ref[0] = _dot(a, w_ref[0].astype(BF)) + b_ref[0]


def _ada_call(c_all, ada_w, ada_b):
    depth = ada_w.shape[0]
    nb = c_all.shape[0]
    return pl.pallas_call(
        _ada_kernel,
        out_shape=jax.ShapeDtypeStruct((depth, nb, 6 * D_MODEL), F32),
        grid=(depth, 6),
        in_specs=[
            pl.BlockSpec((nb, D_MODEL), lambda l, j: (0, 0)),
            pl.BlockSpec((1, D_MODEL, D_MODEL), lambda l, j: (l, 0, j)),
            pl.BlockSpec((1, 1, D_MODEL), lambda l, j: (l, 0, j)),
        ],
        out_specs=pl.BlockSpec((1, nb, D_MODEL), lambda l, j: (l, 0, j)),
        compiler_params=_cparams(("arbitrary", "arbitrary")),
    )(c_all, ada_w, ada_b.reshape(depth, 1, 6 * D_MODEL))


def _proj_kernel(x_ref, sc_ref, sh_ref, g_ref, w_ref, proj_ref, kvb_ref):
    h = _norm_mod(x_ref[0], g_ref[...], sc_ref[0, 0, 0], sh_ref[0, 0, 0]).astype(BF)
    y = _dot(h, w_ref[...])
    proj_ref[0] = y
    kvb_ref[0] = jnp.concatenate([y[:, P_KV:P_KV + KV_W], y[:, P_WIN:P_WIN + WIN_W]], axis=1).astype(BF)


def _mod_spec(l, which, rm):
    return pl.BlockSpec((1, 1, 1, rm, D_MODEL), lambda b, i: (l, b, which, 0, 0))


def _proj_call(l, x3, mod, g, w_packed, tm):
    bx, tx, _ = x3.shape
    rm = mod.shape[3]
    return pl.pallas_call(
        _proj_kernel,
        out_shape=(jax.ShapeDtypeStruct((bx, tx, P_W), F32), jax.ShapeDtypeStruct((bx, tx, KVB_W), BF)),
        grid=(bx, tx // tm),
        in_specs=[
            pl.BlockSpec((1, tm, D_MODEL), lambda b, i: (b, i, 0)),
            _mod_spec(l, 1, rm), _mod_spec(l, 0, rm),
            _const_spec((1, D_MODEL)),
            _const_spec((D_MODEL, P_W)),
        ],
        out_specs=(pl.BlockSpec((1, tm, P_W), lambda b, i: (b, i, 0)),
                   pl.BlockSpec((1, tm, KVB_W), lambda b, i: (b, i, 0))),
        compiler_params=_cparams(("arbitrary", "arbitrary")),
    )(x3, mod, mod, g, w_packed)


def _compress_finish(y, yb, w2):
    n = y.shape[0]
    yn = pltpu.roll(y, n - 1, axis=0)
    hid = CMP_HID
    hk = y[:, 0:hid] + yn[:, hid:2 * hid] + (yb[0:1, 0:hid] + yb[1:2, hid:2 * hid])
    hv = y[:, 2 * hid:3 * hid] + yn[:, 3 * hid:4 * hid] + (yb[0:1, 2 * hid:3 * hid] + yb[1:2, 3 * hid:4 * hid])
    act = jnp.concatenate([jax.nn.gelu(hk), jax.nn.gelu(hv)], axis=1).astype(BF)
    return _dot(act, w2)


def _compress_kernel(ch_ref, wc_ref, pe_ref, w2_ref, o_ref):
    y = _dot(ch_ref[0], wc_ref[...])
    yb = _dot(pe_ref[...], wc_ref[...])
    o_ref[0] = _compress_finish(y, yb, w2_ref[...]).astype(BF)


def _compress_call(chunks, wc, pe2, w2):
    b, n_ch, cw = chunks.shape
    return pl.pallas_call(
        _compress_kernel,
        out_shape=jax.ShapeDtypeStruct((b, n_ch, 2 * NSA_HD), BF),
        grid=(b,),
        in_specs=[pl.BlockSpec((1, n_ch, cw), lambda i: (i, 0, 0)),
                  _const_spec(wc.shape), _const_spec(pe2.shape), _const_spec(w2.shape)],
        out_specs=pl.BlockSpec((1, n_ch, 2 * NSA_HD), lambda i: (i, 0, 0)),
        compiler_params=_cparams(("arbitrary",)),
    )(chunks, wc, pe2, w2)


def _lam_value(lam_ref, lam_init):
    lp = lam_ref[0]
    a = jnp.sum(lp[0:1] * lp[1:2], axis=-1, keepdims=True)
    b = jnp.sum(lp[2:3] * lp[3:4], axis=-1, keepdims=True)
    return jnp.exp(a) - jnp.exp(b) + lam_init


def _group_rms(od, gmat, gain, lam_init):
    ms = _dot(od * od, gmat, precision=HI)
    return od * lax.rsqrt(ms + EPS) * gain * (1.0 - lam_init)


def _prompt_attn_kernel(qa_ref, qd_ref, ga_ref, kvb_ref, kvt_ref, kc_ref, kct_ref, ov_ref, et_ref, gmat_ref,
                        lam_ref, gain_ref, oa_ref, od_ref, m_sc, l_sc, acc_sc, s_sc, p_sc, *, lam_init, top):
    tq = QBLOCK
    kc_len = KEY_CHUNK
    i = pl.program_id(1)
    t0 = i * tq
    n_sel = ov_ref.shape[0]
    slopes = [sl * LOG2E for sl in _slopes(NSA_HEADS)]

    t_row = t0 + lax.broadcasted_iota(jnp.int32, (1, LANES), 1)
    feat = lax.broadcasted_iota(jnp.int32, (LANES, LANES), 0)

    qa = qa_ref[0] * (NSA_HD ** -0.5 * LOG2E)
    qa_t = [qa[:, 0:LANES].T, qa[:, LANES:2 * LANES].T]
    slabs = []
    for h in range(NSA_HEADS):
        tile = qa_t[h // 2]
        if h % 2:
            tile = pltpu.roll(tile, NSA_HD, axis=0)
        slabs.append(jnp.where(feat < NSA_HD, tile, 0.0))
    q_nsa = jnp.concatenate(slabs, axis=1).astype(BF)

    qd = qd_ref[0] * (DIFF_QK ** -0.5 * LOG2E)
    q_diff = []
    for g in range(2):
        tile = qd[:, g * LANES:(g + 1) * LANES].T
        slabs = [jnp.where(lax.shift_right_logical(feat, 5) == sl, tile, 0.0) for sl in range(4)]
        q_diff.append(jnp.concatenate(slabs, axis=1).astype(BF))

    def lanes_of(x, sl):
        return x[:, sl * tq:(sl + 1) * tq]

    kc = kc_ref[0]
    n_ch = kc.shape[0]
    s = _dot(kc, q_nsa)
    cmp_end = lax.broadcasted_iota(jnp.int32, (n_ch, LANES), 0) * CMP_STRIDE + (CMP_BLOCK - 1)
    vis = cmp_end <= t_row
    p_heads = []
    for h in range(NSA_HEADS):
        sh = jnp.where(vis, lanes_of(s, h), NEG)
        m = jnp.max(sh, axis=0, keepdims=True)
        e = jnp.where(vis, jnp.exp2(sh - m), 0.0)
        d = jnp.sum(e, axis=0, keepdims=True)
        p_heads.append(e / jnp.where(d > 0, d, 1.0))
    o_cmp = _dot(kct_ref[0], jnp.concatenate(p_heads, axis=1).astype(BF))
    p_sum = p_heads[0] + p_heads[1] + p_heads[2] + p_heads[3]
    imp_t = _dot(ov_ref[...], p_sum, precision=HI)

    j_full = lax.broadcasted_iota(jnp.int32, (n_sel, LANES), 0)
    valid = j_full * SEL_BLOCK <= t_row
    cur = lax.shift_right_logical(t_row, 6)
    forced = jnp.where(j_full == 0, 1, jnp.where(j_full == cur, 1, jnp.where(j_full == cur - 1, 1, 0)))
    score = jnp.where(valid, jnp.where(forced > 0, BIG, imp_t), -BIG)
    rank = jnp.zeros((n_sel, LANES), F32)
    for ii in range(n_sel):
        si = score[ii:ii + 1, :]
        rank = rank + jnp.where(si > score, 1.0, jnp.where(si == score, jnp.where(j_full > ii, 1.0, 0.0), 0.0))
    sel = jnp.where(rank < top, 1.0, 0.0).astype(BF)

    m_sc[...] = jnp.full(m_sc.shape, NEG, F32)
    l_sc[...] = jnp.zeros(l_sc.shape, F32)
    acc_sc[...] = jnp.zeros(acc_sc.shape, F32)

    k_tile = [1, 2, 3]
    v_tile = [1, 4, 5]
    q_grp = [q_nsa, q_diff[0], q_diff[1]]
    n_chunks = (t0 + tq + kc_len - 1) // kc_len

    def chunk_body(c, carry):
        ks = pl.multiple_of(c * kc_len, kc_len)
        for grp in range(3):
            s_sc[grp] = _dot(kvb_ref[0, pl.ds(ks, kc_len), k_tile[grp] * LANES:(k_tile[grp] + 1) * LANES], q_grp[grp])
        pos = ks + lax.broadcasted_iota(jnp.int32, (kc_len, LANES), 0)
        rel = (pos - t0).astype(F32)
        causal = pos <= t_row
        sel_exp = _dot(et_ref[pl.ds(ks, kc_len), :], sel)
        mb_sel = jnp.where(causal, jnp.where(sel_exp > 0.5, 0.0, NEG), NEG)
        mb_d = jnp.where(causal, 0.0, NEG)
        alibi = [slopes[h] * rel for h in range(NSA_HEADS)]
        bias_s = [a + mb_sel for a in alibi]
        bias_d = [a + mb_d for a in alibi]
        alpha_grp = []
        for grp in range(3):
            m_old_all = m_sc[grp]
            l_old_all = l_sc[grp]
            alphas, ms, ls = [], [], []
            for sl in range(4):
                bias = bias_s[sl] if grp == 0 else bias_d[2 * (grp - 1) + sl // 2]
                s_h = s_sc[grp, :, sl * tq:(sl + 1) * tq] + bias
                m_old = lanes_of(m_old_all, sl)
                m_new = jnp.maximum(m_old, jnp.max(s_h, axis=0, keepdims=True))
                alpha = jnp.exp2(m_old - m_new)
                e = jnp.exp2(s_h - m_new)
                ls.append(alpha * lanes_of(l_old_all, sl) + jnp.sum(e, axis=0, keepdims=True))
                ms.append(m_new)
                alphas.append(alpha)
                p_sc[grp, :, sl * tq:(sl + 1) * tq] = e.astype(BF)
            m_sc[grp] = jnp.concatenate(ms, axis=1)
            l_sc[grp] = jnp.concatenate(ls, axis=1)
            alpha_grp.append(jnp.concatenate(alphas, axis=1))
        for grp in range(3):
            v_t = kvt_ref[0, v_tile[grp] * LANES:(v_tile[grp] + 1) * LANES, pl.ds(ks, kc_len)]
            acc_sc[grp] = alpha_grp[grp] * acc_sc[grp] + _dot(v_t, p_sc[grp])
        return carry

    lax.fori_loop(0, n_chunks, chunk_body, 0)

    o_sel = acc_sc[0] / l_sc[0]

    span = WINDOW + tq
    start = pl.multiple_of(jnp.maximum(i - WINDOW // tq, 0) * tq, tq)
    pos_w = start + lax.broadcasted_iota(jnp.int32, (span, LANES), 0)
    dist = t_row - pos_w
    mb_w = jnp.where(dist >= 0, jnp.where(dist < WINDOW, 0.0, NEG), NEG)
    rel_w = (pos_w - t0).astype(F32)
    sw = _dot(kvb_ref[0, pl.ds(start, span), KV_W:KV_W + WIN_W], q_nsa)
    pw = []
    for h in range(NSA_HEADS):
        s_h = lanes_of(sw, h) + slopes[h] * rel_w + mb_w
        e = jnp.exp2(s_h - jnp.max(s_h, axis=0, keepdims=True))
        pw.append((e / jnp.sum(e, axis=0, keepdims=True)).astype(BF))
    o_win = _dot(kvt_ref[0, KV_W:KV_W + WIN_W, pl.ds(start, span)], jnp.concatenate(pw, axis=1))

    ga_t = jax.nn.sigmoid(ga_ref[0]).T
    oa_h = []
    for h in range(NSA_HEADS):
        oa_h.append(ga_t[3 * h:3 * h + 1] * lanes_of(o_cmp, h) + ga_t[3 * h + 1:3 * h + 2] * lanes_of(o_sel, h)
                    + ga_t[3 * h + 2:3 * h + 3] * lanes_of(o_win, h))
    oa_ref[0] = jnp.concatenate(
        [jnp.concatenate([oa_h[0][NSA_HD:], oa_h[1][NSA_HD:]], axis=0).T,
         jnp.concatenate([oa_h[2][NSA_HD:], oa_h[3][NSA_HD:]], axis=0).T], axis=1)

    lam = _lam_value(lam_ref, lam_init)
    tiles = []
    for g in range(2):
        on = acc_sc[1 + g] / l_sc[1 + g]
        a0 = lanes_of(on, 0) - lam * lanes_of(on, 1)
        a1 = lanes_of(on, 2) - lam * lanes_of(on, 3)
        tiles.append(jnp.concatenate([a0[:DIFF_V], a1[DIFF_V:]], axis=0).T)
    od = jnp.concatenate(tiles, axis=1)
    od_ref[0] = _group_rms(od, gmat_ref[...], gain_ref[0], lam_init)


def _prompt_attn_call(l, proj, kvb, kvt, kcvc, kct, ov, et, gmat, diff_lam, gain, lam_init):
    b, t, _ = proj.shape
    n_ch = kcvc.shape[1]
    n_sel = ov.shape[0]
    tq = QBLOCK
    kern = functools.partial(_prompt_attn_kernel, lam_init=lam_init, top=min(SEL_TOP_N, n_sel))
    return pl.pallas_call(
        kern,
        out_shape=(jax.ShapeDtypeStruct((b, t, BRANCH_W), F32), jax.ShapeDtypeStruct((b, t, BRANCH_W), F32)),
        grid=(b, t // tq),
        in_specs=[
            pl.BlockSpec((1, tq, 256), lambda bi, i: (bi, i, P_QA // 256)),
            pl.BlockSpec((1, tq, 256), lambda bi, i: (bi, i, P_QD // 256)),
            pl.BlockSpec((1, tq, LANES), lambda bi, i: (bi, i, P_GA // LANES)),
            pl.BlockSpec((1, t, KVB_W), lambda bi, i: (bi, 0, 0)),
            pl.BlockSpec((1, KVB_W, t), lambda bi, i: (bi, 0, 0)),
            pl.BlockSpec((1, n_ch, 2 * NSA_HD), lambda bi, i: (bi, 0, 0)),
            pl.BlockSpec((1, 2 * NSA_HD, n_ch), lambda bi, i: (bi, 0, 0)),
            _const_spec(ov.shape), _const_spec(et.shape), _const_spec(gmat.shape),
            pl.BlockSpec((1, 4, DIFF_QK), lambda bi, i: (l, 0, 0)),
            pl.BlockSpec((1, 1, BRANCH_W), lambda bi, i: (l, 0, 0)),
        ],
        out_specs=(pl.BlockSpec((1, tq, BRANCH_W), lambda bi, i: (bi, i, 0)),
                   pl.BlockSpec((1, tq, BRANCH_W), lambda bi, i: (bi, i, 0))),
        scratch_shapes=[pltpu.VMEM((3, 1, 4 * tq), F32), pltpu.VMEM((3, 1, 4 * tq), F32),
                        pltpu.VMEM((3, LANES, 4 * tq), F32),
                        pltpu.VMEM((3, KEY_CHUNK, 4 * tq), F32), pltpu.VMEM((3, KEY_CHUNK, 4 * tq), BF)],
        compiler_params=_cparams(("arbitrary", "arbitrary")),
    )(proj, proj, proj, kvb, kvt, kcvc, kct, ov, et, gmat, diff_lam, gain)


def _sample_attn_kernel(pt_ref, proj_ref, *refs, n_pages, lam_init, top, n_sel):
    del pt_ref
    pages = refs[:n_pages]
    (swin_ref, wc_ref, pe_ref, w2_ref, ov_ref, e_ref, gmat_ref, lam_ref, gain_ref, oa_ref, od_ref,
     cmp_rows) = refs[n_pages:]
    past = n_pages * PAGE_SIZE
    n_chp = past // CMP_STRIDE
    n_vis = (past - (CMP_BLOCK - 1)) // CMP_STRIDE + 1
    cur = past // SEL_BLOCK
    nsa_slopes = _slopes(NSA_HEADS)
    diff_slopes = _slopes(DIFF_HEADS)

    row = proj_ref[0]
    kv_new = row[:, P_KV:P_KV + KV_W]
    lane = lax.broadcasted_iota(jnp.int32, (1, LANES), 1)
    lo = lane < NSA_HD
    rid = lax.broadcasted_iota(jnp.int32, (SUBLANES, 1), 0)
    rid_full = lax.broadcasted_iota(jnp.int32, (SUBLANES, LANES), 0)

    def bcast(v):
        return jnp.broadcast_to(v, (SUBLANES, v.shape[1]))

    qa = row[:, P_QA:P_QA + 256] * (NSA_HD ** -0.5)
    b0, b1 = bcast(qa[:, 0:LANES]), bcast(qa[:, LANES:2 * LANES])
    cand = [b0, pltpu.roll(b0, NSA_HD, axis=1), b1, pltpu.roll(b1, NSA_HD, axis=1)]
    q8 = jnp.zeros((SUBLANES, LANES), F32)
    for h in range(NSA_HEADS):
        q8 = jnp.where(rid_full == h, cand[h], q8)
    q8 = jnp.where(lo, q8, 0.0)
    q8b = q8.astype(BF)
    slope_col = jnp.zeros((SUBLANES, 1), F32)
    for h in range(NSA_HEADS):
        slope_col = jnp.where(rid == h, nsa_slopes[h], slope_col)

    for j, pg in enumerate(pages):
        cmp_rows[j * PAGE_SIZE:(j + 1) * PAGE_SIZE, :] = pg[0, 0, :, 0:LANES]
    y = jnp.zeros((n_chp, 4 * CMP_HID), F32)
    for r in range(CMP_STRIDE):
        xr = cmp_rows[pl.ds(r, n_chp, stride=CMP_STRIDE), :].astype(BF)
        y = y + _dot(xr, wc_ref[r * LANES:(r + 1) * LANES, :])
    yb = _dot(pe_ref[...], wc_ref[...])
    kc = _compress_finish(y, yb, w2_ref[...]).astype(BF)

    cl = lax.broadcasted_iota(jnp.int32, (1, n_chp), 1)
    vis = cl < n_vis
    s = jnp.where(vis, _dot_nt(q8b, kc), NEG)
    m = jnp.max(s, axis=-1, keepdims=True)
    e = jnp.where(vis, jnp.exp(s - m), 0.0)
    d = jnp.sum(e, axis=-1, keepdims=True)
    p_c = jnp.where(rid < NSA_HEADS, e / jnp.where(d > 0, d, 1.0), 0.0)
    o_cmp = _dot(p_c.astype(BF), kc)
    imp = jnp.sum(_dot(p_c, ov_ref[...], precision=HI), axis=0, keepdims=True)

    forced = jnp.where(lane == 0, 1, jnp.where(lane == cur, 1, jnp.where(lane == cur - 1, 1, 0)))
    score = jnp.where(lane < n_sel, jnp.where(forced > 0, BIG, imp), -BIG)
    r_row = jnp.broadcast_to(score, (LANES, LANES))
    r_col = r_row.T
    ii = lax.broadcasted_iota(jnp.int32, (LANES, LANES), 0)
    jj = lax.broadcasted_iota(jnp.int32, (LANES, LANES), 1)
    inc = jnp.where(r_col > r_row, 1.0, jnp.where(r_col == r_row, jnp.where(ii < jj, 1.0, 0.0), 0.0))
    rank = jnp.sum(inc, axis=0, keepdims=True)
    sel = jnp.where(rank < top, 1.0, 0.0)
    sel_exp = _dot(bcast(sel).astype(BF), e_ref[...])

    pos = lax.broadcasted_iota(jnp.int32, (1, past), 1)
    rel = (pos - past).astype(F32)

    def attend_new(q, sc, k_tile, kv_tile_new):
        s_new = jnp.sum(q * kv_tile_new, axis=-1, keepdims=True)
        mm = jnp.maximum(jnp.max(sc, axis=-1, keepdims=True), s_new)
        ee = jnp.exp(sc - mm)
        e_new = jnp.exp(s_new - mm)
        den = jnp.sum(ee, axis=-1, keepdims=True) + e_new
        return (_dot(ee.astype(BF), k_tile) + e_new * kv_tile_new) / den

    k_sel = jnp.concatenate([pg[0, 0, :, LANES:2 * LANES] for pg in pages], axis=0).astype(BF)
    sc = _dot_nt(q8b, k_sel) + slope_col * rel + jnp.where(sel_exp > 0.5, 0.0, NEG)
    o_sel = attend_new(q8, sc, k_sel, kv_new[:, LANES:2 * LANES])

    k_win = swin_ref[0, 0].astype(BF)
    nw = k_win.shape[0]
    wl = lax.broadcasted_iota(jnp.int32, (1, nw), 1)
    dist_w = nw - wl
    sc = _dot_nt(q8b, k_win) - slope_col * dist_w.astype(F32) + jnp.where(dist_w < WINDOW, 0.0, NEG)
    o_win = attend_new(q8, sc, k_win, row[:, P_WIN:P_WIN + WIN_W])

    ga8 = bcast(jax.nn.sigmoid(row[:, P_GA:P_GA + LANES]))
    gates = [jnp.sum(jnp.where(lane == 3 * rid_full + k, ga8, 0.0), axis=-1, keepdims=True) for k in range(3)]
    oa8 = gates[0] * o_cmp + gates[1] * o_sel + gates[2] * o_win
    oa_ref[0] = jnp.concatenate(
        [jnp.where(lo, pltpu.roll(oa8[0:1], NSA_HD, axis=1), oa8[1:2]),
         jnp.where(lo, pltpu.roll(oa8[2:3], NSA_HD, axis=1), oa8[3:4])], axis=1)

    qd = bcast(row[:, P_QD:P_QD + 256] * (DIFF_QK ** -0.5))
    lane2 = lax.broadcasted_iota(jnp.int32, (SUBLANES, 2 * LANES), 1)
    rid2 = lax.broadcasted_iota(jnp.int32, (SUBLANES, 2 * LANES), 0)
    qd8 = jnp.where(lax.shift_right_logical(lane2, 5) == rid2, qd, 0.0)
    dslope = jnp.zeros((SUBLANES, 1), F32)
    for r in range(SUBLANES):
        dslope = jnp.where(rid == r, diff_slopes[r // 2], dslope)
    k_d = jnp.concatenate([pg[0, 0, :, 2 * LANES:4 * LANES] for pg in pages], axis=0).astype(BF)
    v_d = jnp.concatenate([pg[0, 0, :, 4 * LANES:6 * LANES] for pg in pages], axis=0).astype(BF)
    sc = _dot_nt(qd8.astype(BF), k_d) + dslope * rel
    s_new = jnp.sum(qd8 * kv_new[:, 2 * LANES:4 * LANES], axis=-1, keepdims=True)
    mm = jnp.maximum(jnp.max(sc, axis=-1, keepdims=True), s_new)
    ee = jnp.exp(sc - mm)
    e_new = jnp.exp(s_new - mm)
    den = jnp.sum(ee, axis=-1, keepdims=True) + e_new
    o8 = (_dot(ee.astype(BF), v_d) + e_new * kv_new[:, 4 * LANES:6 * LANES]) / den
    lam = _lam_value(lam_ref, lam_init)
    coef = jnp.where((rid2 & 1) == 0, 1.0, -lam)
    head_mask = lax.shift_right_logical(lane2, 6) == lax.shift_right_logical(rid2, 1)
    od = jnp.sum(jnp.where(head_mask, coef * o8, 0.0), axis=0, keepdims=True)
    od8 = _group_rms(bcast(od), gmat_ref[...], gain_ref[0], lam_init)
    od_ref[0] = od8[0:1]


def _sample_attn_call(l, page_table, proj, cache_kv, state_win, wc, pe2, w2, ov, emat, gmat, diff_lam, gain,
                      lam_init, n_sel):
    bs = proj.shape[0]
    n_pages = page_table.shape[1]
    nw = state_win.shape[2]
    kern = functools.partial(_sample_attn_kernel, n_pages=n_pages, lam_init=lam_init,
                             top=min(SEL_TOP_N, n_sel), n_sel=n_sel)

    def page_spec(j):
        return pl.BlockSpec((1, 1, PAGE_SIZE, KV_W), lambda b, pt: (l, pt[b, j], 0, 0))

    def cspec(shape):
        nd = len(shape)
        return pl.BlockSpec(shape, lambda b, pt: (0,) * nd, pipeline_mode=pl.Buffered(1))

    grid_spec = pltpu.PrefetchScalarGridSpec(
        num_scalar_prefetch=1,
        grid=(bs,),
        in_specs=[pl.BlockSpec((1, 1, P_W), lambda b, pt: (b, 0, 0))]
        + [page_spec(j) for j in range(n_pages)]
        + [pl.BlockSpec((1, 1, nw, WIN_W), lambda b, pt: (l, b, 0, 0)),
           cspec(wc.shape), cspec(pe2.shape), cspec(w2.shape), cspec(ov.shape), cspec(emat.shape),
           cspec(gmat.shape),
           pl.BlockSpec((1, 4, DIFF_QK), lambda b, pt: (l, 0, 0)),
           pl.BlockSpec((1, 1, BRANCH_W), lambda b, pt: (l, 0, 0))],
        out_specs=(pl.BlockSpec((1, 1, BRANCH_W), lambda b, pt: (b, 0, 0)),
                   pl.BlockSpec((1, 1, BRANCH_W), lambda b, pt: (b, 0, 0))),
        scratch_shapes=[pltpu.VMEM((n_pages * PAGE_SIZE, LANES), F32)],
    )
    return pl.pallas_call(
        kern,
        out_shape=(jax.ShapeDtypeStruct((bs, 1, BRANCH_W), F32), jax.ShapeDtypeStruct((bs, 1, BRANCH_W), F32)),
        grid_spec=grid_spec,
        compiler_params=_cparams(("arbitrary",)),
    )(page_table, proj, *([cache_kv] * n_pages), state_win, wc, pe2, w2, ov, emat, gmat, diff_lam, gain)


def _merge_kernel(*refs, halo):
    if halo:
        (x_ref, sc_ref, sh_ref, gt_ref, g_ref, cv_ref, ub_ref, cvp_ref, ubp_ref, oa_ref, od_ref,
         wg_ref, wb_ref, wo_ref, pw_ref, ps_ref, cw_ref, xo_ref, zin_ref) = refs
    else:
        (x_ref, sc_ref, sh_ref, gt_ref, g_ref, cv_ref, ub_ref, stc_ref, stp_ref, oa_ref, od_ref,
         wg_ref, wb_ref, wo_ref, pw_ref, ps_ref, cw_ref, xo_ref, zin_ref) = refs
    i = pl.program_id(1)
    x = x_ref[0]
    tm = x.shape[0]
    h = _norm_mod(x, g_ref[...], sc_ref[0, 0, 0], sh_ref[0, 0, 0]).astype(BF)

    cv = cv_ref[0]
    u = ub_ref[0]
    bg = cv[:, 0:CONV_W]
    zin = cv[:, CONV_W:2 * CONV_W] * cv[:, 2 * CONV_W:3 * CONV_W]
    zin_ref[0] = zin
    cw = cw_ref[...]
    lane = lax.broadcasted_iota(jnp.int32, (1, POOL_W), 1)
    win_lane = jnp.where(lane < 64, 2, jnp.where(lane < 128, 4, jnp.where(lane < 192, 8, 16)))

    if halo:
        keep = jnp.where(i > 0, 1.0, 0.0)
        cvp = cvp_ref[0] * keep
        ubp = ubp_ref[0] * keep
        z_ext = jnp.concatenate([cvp[:, CONV_W:2 * CONV_W] * cvp[:, 2 * CONV_W:3 * CONV_W], zin], axis=0)
        z = (cw[0:1] * pltpu.roll(z_ext, 2, axis=0) + cw[1:2] * pltpu.roll(z_ext, 1, axis=0)
             + cw[2:3] * z_ext)[HALO:]
        ext = jnp.concatenate([ubp, u], axis=0)
        s2 = ext + pltpu.roll(ext, 1, axis=0)
        s4 = s2 + pltpu.roll(s2, 2, axis=0)
        s8 = s4 + pltpu.roll(s4, 4, axis=0)
        s16 = s8 + pltpu.roll(s8, 8, axis=0)
        tot = jnp.where(lane < 64, s2, jnp.where(lane < 128, s4, jnp.where(lane < 192, s8, s16)))[HALO:]
        pos1 = i * tm + lax.broadcasted_iota(jnp.int32, (tm, 1), 0) + 1
        cnt = jnp.minimum(win_lane, pos1).astype(F32)
    else:
        z = cw[0:1] * stc_ref[0] + cw[1:2] * stc_ref[1] + cw[2:3] * zin
        s2 = u + stp_ref[POOL_KEEP - 1]
        s4 = s2 + stp_ref[POOL_KEEP - 2] + stp_ref[POOL_KEEP - 3]
        s8 = s4
        for k in range(4, 8):
            s8 = s8 + stp_ref[POOL_KEEP - k]
        s16 = s8
        for k in range(8, 16):
            s16 = s16 + stp_ref[POOL_KEEP - k]
        tot = jnp.where(lane < 64, s2, jnp.where(lane < 128, s4, jnp.where(lane < 192, s8, s16)))
        cnt = win_lane.astype(F32)
    o_c = bg * z
    pooled = tot / cnt - u
    o_b = _dot(pooled.astype(BF), pw_ref[...]) * ps_ref[...]

    branches = [oa_ref[0], o_b, o_c, od_ref[0]]
    mixed = jnp.zeros((tm, D_MODEL), F32)
    for n in range(N_BRANCH):
        pb = _dot(branches[n].astype(BF), wb_ref[n])
        gate = jax.nn.sigmoid(_dot(h, wg_ref[:, n * D_MODEL:(n + 1) * D_MODEL]))
        mixed = mixed + gate * pb
    xo_ref[0] = x + gt_ref[0, 0, 0] * _dot(mixed.astype(BF), wo_ref[...])


def _merge_call(l, x3, mod, g, proj, prev_c, prev_p, o_a, o_d, wg, wb, wo, pw, ps, cw, tm, halo):
    bx, tx, _ = x3.shape
    rm = mod.shape[3]
    if halo:
        nblk = tm // HALO
        prev_specs = [
            pl.BlockSpec((1, HALO, 3 * CONV_W), lambda b, i: (b, jnp.maximum(i * nblk - 1, 0), P_CV // (3 * CONV_W))),
            pl.BlockSpec((1, HALO, POOL_W), lambda b, i: (b, jnp.maximum(i * nblk - 1, 0), P_UB // POOL_W)),
        ]
    else:
        prev_specs = [pl.BlockSpec(prev_c.shape, lambda b, i: (0, 0, 0)),
                      pl.BlockSpec(prev_p.shape, lambda b, i: (0, 0, 0))]
    return pl.pallas_call(
        functools.partial(_merge_kernel, halo=halo),
        out_shape=(jax.ShapeDtypeStruct((bx, tx, D_MODEL), F32), jax.ShapeDtypeStruct((bx, tx, CONV_W), F32)),
        grid=(bx, tx // tm),
        in_specs=[
            pl.BlockSpec((1, tm, D_MODEL), lambda b, i: (b, i, 0)),
            _mod_spec(l, 1, rm), _mod_spec(l, 0, rm), _mod_spec(l, 2, rm),
            _const_spec((1, D_MODEL)),
            pl.BlockSpec((1, tm, 3 * CONV_W), lambda b, i: (b, i, P_CV // (3 * CONV_W))),
            pl.BlockSpec((1, tm, POOL_W), lambda b, i: (b, i, P_UB // POOL_W)),
            *prev_specs,
            pl.BlockSpec((1, tm, BRANCH_W), lambda b, i: (b, i, 0)),
            pl.BlockSpec((1, tm, BRANCH_W), lambda b, i: (b, i, 0)),
            _const_spec(wg.shape), _const_spec(wb.shape), _const_spec(wo.shape),
            _const_spec(pw.shape), _const_spec(ps.shape), _const_spec(cw.shape),
        ],
        out_specs=(pl.BlockSpec((1, tm, D_MODEL), lambda b, i: (b, i, 0)),
                   pl.BlockSpec((1, tm, CONV_W), lambda b, i: (b, i, 0))),
        compiler_params=_cparams(("arbitrary", "arbitrary")),
    )(x3, mod, mod, mod, g, proj, proj, prev_c, prev_p, o_a, o_d, wg, wb, wo, pw, ps, cw)


def _ffn_kernel(*refs, halo, final, keep_rows):
    (x_ref, sc_ref, sh_ref, gt_ref, g_ref, prev_ref, wu_ref, fc_ref, wd_ref, fg_ref, xo_ref, up_ref) = refs
    i = pl.program_id(1)
    x = x_ref[0]
    tm = x.shape[0]
    g = g_ref[...]
    sc = sc_ref[0, 0, 0]
    sh = sh_ref[0, 0, 0]
    if halo:
        hx = jnp.concatenate([prev_ref[0], x], axis=0)
        h2 = _norm_mod(hx, g, sc, sh).astype(BF)
        rows = lax.broadcasted_iota(jnp.int32, (tm + FFN_HALO, 1), 0)
        live = jnp.where(rows >= FFN_HALO, 1.0, jnp.where(i > 0, 1.0, 0.0))
    else:
        h2 = _norm_mod(x, g, sc, sh).astype(BF)
    acc = jnp.zeros((tm, D_MODEL), F32)
    for c in range(D_FF // FF_CHUNK):
        halves = []
        for part in range(2):
            col = part * D_FF + c * FF_CHUNK
            up = _dot(h2, wu_ref[:, col:col + FF_CHUNK])
            w = fc_ref[:, col:col + FF_CHUNK]
            if halo:
                up = up * live
                up_ref[0, :, col:col + FF_CHUNK] = up[FFN_HALO + tm - keep_rows:]
                upc = (w[0:1] * pltpu.roll(up, 2, axis=0) + w[1:2] * pltpu.roll(up, 1, axis=0)
                       + w[2:3] * up)[FFN_HALO:]
            else:
                up_ref[0, :, col:col + FF_CHUNK] = up
                upc = (w[0:1] * prev_ref[0, :, col:col + FF_CHUNK] + w[1:2] * prev_ref[1, :, col:col + FF_CHUNK]
                       + w[2:3] * up)
            halves.append(upc)
        val, gg = halves
        act = (gg * jax.nn.sigmoid(gg) * val).astype(BF)
        acc = acc + _dot(act, wd_ref[c * FF_CHUNK:(c + 1) * FF_CHUNK, :])
    out = x + gt_ref[0, 0, 0] * acc
    if final:
        ms = jnp.mean(out * out, axis=-1, keepdims=True)
        out = out * lax.rsqrt(ms + EPS) * fg_ref[...]
    xo_ref[0] = out


def _ffn_call(l, x3, mod, g, prev, wu, fc, wd, fg, tm, halo, final):
    bx, tx, _ = x3.shape
    rm = mod.shape[3]
    keep_rows = SUBLANES if halo else tm
    if halo:
        nblk = tm // FFN_HALO
        prev_spec = pl.BlockSpec((1, FFN_HALO, D_MODEL), lambda b, i: (b, jnp.maximum(i * nblk - 1, 0), 0))
    else:
        prev_spec = pl.BlockSpec(prev.shape, lambda b, i: (0, 0, 0))
    return pl.pallas_call(
        functools.partial(_ffn_kernel, halo=halo, final=final, keep_rows=keep_rows),
        out_shape=(jax.ShapeDtypeStruct((bx, tx, D_MODEL), F32),
                   jax.ShapeDtypeStruct((bx, keep_rows, 2 * D_FF), F32)),
        grid=(bx, tx // tm),
        in_specs=[
            pl.BlockSpec((1, tm, D_MODEL), lambda b, i: (b, i, 0)),
            _mod_spec(l, 4, rm), _mod_spec(l, 3, rm), _mod_spec(l, 5, rm),
            _const_spec((1, D_MODEL)),
            prev_spec,
            _const_spec(wu.shape), _const_spec(fc.shape), _const_spec(wd.shape), _const_spec((1, D_MODEL)),
        ],
        out_specs=(pl.BlockSpec((1, tm, D_MODEL), lambda b, i: (b, i, 0)),
                   pl.BlockSpec((1, keep_rows, 2 * D_FF), lambda b, i: (b, 0, 0))),
        compiler_params=_cparams(("arbitrary", "arbitrary")),
    )(x3, mod, mod, mod, g, prev, wu, fc, wd, fg)


def _pack_w_in(w_in):
    o = np.cumsum([0, 256, KV_W, WIN_W, GA_W, POOL_W, 3 * CONV_W, 256, N_BRANCH * D_MODEL])
    q_a, kv, win, g_a, u_b, cv, q_d, gates = [w_in[..., o[k]:o[k + 1]] for k in range(8)]
    g_a = jnp.pad(g_a, ((0, 0), (0, 0), (0, LANES - GA_W)))
    packed = jnp.concatenate([kv, cv, q_a, u_b, q_d, win, g_a], axis=-1).astype(BF)
    return packed, gates.astype(BF)


def _compress_weights(cmp_pe, cmp_w1, cmp_w2):
    depth = cmp_w1.shape[0]
    half = CMP_STRIDE * NSA_HD
    w1 = cmp_w1.reshape(depth, 2, 2, CMP_STRIDE, NSA_HD, CMP_HID)
    z = jnp.zeros((depth, CMP_STRIDE, NSA_HD, CMP_HID), cmp_w1.dtype)
    cols = []
    for kv in range(2):
        for hf in range(2):
            blk = w1[:, kv, hf]
            pair = (blk, z) if kv == 0 else (z, blk)
            cols.append(jnp.concatenate(pair, axis=2).reshape(depth, 2 * half, CMP_HID))
    wc = jnp.concatenate(cols, axis=-1).astype(BF)
    pe = cmp_pe.reshape(depth, 2, 2, CMP_STRIDE, NSA_HD)
    pe2 = jnp.concatenate([pe[:, 0], pe[:, 1]], axis=-1).reshape(depth, 2, 2 * half)
    pe2 = jnp.pad(pe2, ((0, 0), (0, SUBLANES - 2), (0, 0))).astype(BF)
    zz = jnp.zeros((depth, CMP_HID, NSA_HD), cmp_w2.dtype)
    w2 = jnp.concatenate([jnp.concatenate([cmp_w2[:, 0], zz], axis=-1),
                          jnp.concatenate([zz, cmp_w2[:, 1]], axis=-1)], axis=1).astype(BF)
    return wc, pe2, w2


def _overlap(n_cmp, n_sel):
    cs = np.arange(n_cmp)[:, None] * CMP_STRIDE
    ss = np.arange(n_sel)[None, :] * SEL_BLOCK
    ov = np.minimum(cs + CMP_BLOCK, ss + SEL_BLOCK) - np.maximum(cs, ss)
    return (np.maximum(ov, 0) / CMP_BLOCK).astype(np.float32)


def _expand_matrix(n_rows, n_pos):
    return (np.arange(n_pos)[None, :] // SEL_BLOCK == np.arange(n_rows)[:, None]).astype(np.float32)


def _group_mean_matrix():
    idx = np.arange(DIFF_HEADS * DIFF_V) // DIFF_V
    return (idx[:, None] == idx[None, :]).astype(np.float32) / DIFF_V


def _pool_blockdiag(pool_w):
    depth, n, g, _ = pool_w.shape
    out = jnp.zeros((depth, n * g, n * g), pool_w.dtype)
    for k in range(n):
        out = out.at[:, k * g:(k + 1) * g, k * g:(k + 1) * g].set(pool_w[:, k])
    return out.astype(BF)


def kernel(x_prompt, x_sample, cache_kv, state_win_kv, state_pool, state_conv, state_ffn, page_table, c_prompt, c_sample, norm_g, ada_w, ada_b, w_in, cmp_pe, cmp_w1, cmp_w2, diff_lam, diff_norm_g, pool_w, pool_scale, conv_w, w_branch, w_out, w_up, ffn_conv, w_down, final_g):
    depth = w_in.shape[0]
    bp, seq, _ = x_prompt.shape
    bs = x_sample.shape[0]
    n_pages = page_table.shape[1]
    past = n_pages * PAGE_SIZE
    tm = min(ROW_TILE, seq)
    assert seq % tm == 0 and seq % QBLOCK == 0 and seq >= WINDOW + QBLOCK and x_sample.shape[1] == 1

    w_packed, w_gate = _pack_w_in(w_in)
    wc, pe2, w2 = _compress_weights(cmp_pe, cmp_w1, cmp_w2)
    wb = w_branch.astype(BF)
    wo = w_out.astype(BF)
    wu = w_up.astype(BF)
    wd = w_down.astype(BF)
    pw = _pool_blockdiag(pool_w)
    ps = pool_scale.reshape(depth, 1, POOL_W)
    gain = jnp.tile(diff_norm_g, (1, DIFF_HEADS)).reshape(depth, 1, DIFF_HEADS * DIFF_V)
    fg = final_g.reshape(1, D_MODEL)
    gmat = jnp.asarray(_group_mean_matrix())

    n_ch = seq // CMP_STRIDE
    n_sel_p = seq // SEL_BLOCK
    ov_p = np.zeros((n_sel_p, n_ch), np.float32)
    ov_p[:, :n_ch - 1] = _overlap(n_ch - 1, n_sel_p).T
    ov_p = jnp.asarray(ov_p)
    et_p = jnp.asarray(_expand_matrix(n_sel_p, seq).T, dtype=BF)
    n_sel_s = -(-(past + 1) // SEL_BLOCK)
    n_chp = past // CMP_STRIDE
    ov_s = np.zeros((n_chp, LANES), np.float32)
    ov_s[:, :n_sel_s] = _overlap(n_chp, n_sel_s)
    ov_s = jnp.asarray(ov_s)
    emat_s = jnp.asarray(_expand_matrix(LANES, past), dtype=BF)

    mod = _ada_call(jnp.concatenate([c_prompt, c_sample], axis=0), ada_w, ada_b)
    mod_p = mod[:, :bp].reshape(depth, bp, 6, 1, D_MODEL)
    mod_s = jnp.transpose(mod[:, bp:].reshape(depth, bs, 6, D_MODEL), (0, 2, 1, 3)).reshape(depth, 1, 6, bs, D_MODEL)

    x = x_prompt
    kv_p, win_p, pool_p, conv_p, ffn_p = [], [], [], [], []
    for l in range(depth):
        lam_init = 0.8 - 0.6 * math.exp(-0.3 * l)
        g1 = norm_g[l, 0].reshape(1, D_MODEL)
        g2 = norm_g[l, 1].reshape(1, D_MODEL)
        proj, kvb = _proj_call(l, x, mod_p, g1, w_packed[l], tm)
        chunks = kvb[:, :, 0:2 * NSA_HD].reshape(bp, n_ch, CMP_STRIDE * 2 * NSA_HD)
        kcvc = _compress_call(chunks, wc[l], pe2[l], w2[l])
        o_a, o_d = _prompt_attn_call(l, proj, kvb, jnp.swapaxes(kvb, 1, 2), kcvc, jnp.swapaxes(kcvc, 1, 2),
                                     ov_p, et_p, gmat, diff_lam, gain, lam_init)
        x, zin = _merge_call(l, x, mod_p, g1, proj, proj, proj, o_a, o_d, w_gate[l], wb[l], wo[l], pw[l], ps[l],
                             conv_w[l], tm, True)
        x, up_last = _ffn_call(l, x, mod_p, g2, x, wu[l], ffn_conv[l], wd[l], fg, tm, True, l == depth - 1)
        kv_p.append(proj[:, :, P_KV:P_KV + KV_W])
        win_p.append(proj[:, seq - min(WINDOW, seq):, P_WIN:P_WIN + WIN_W])
        pool_p.append(proj[:, seq - POOL_KEEP:, P_UB:P_UB + POOL_W])
        conv_p.append(zin[:, seq - 2:])
        ffn_p.append(up_last[:, SUBLANES - 2:])
    y_prompt = x

    x = x_sample.reshape(1, bs, D_MODEL)
    kv_s, win_s, pool_s, conv_s, ffn_s = [], [], [], [], []
    for l in range(depth):
        lam_init = 0.8 - 0.6 * math.exp(-0.3 * l)
        g1 = norm_g[l, 0].reshape(1, D_MODEL)
        g2 = norm_g[l, 1].reshape(1, D_MODEL)
        proj, _ = _proj_call(l, x, mod_s, g1, w_packed[l], bs)
        proj_rows = proj.reshape(bs, 1, P_W)
        o_a, o_d = _sample_attn_call(l, page_table, proj_rows, cache_kv, state_win_kv, wc[l], pe2[l], w2[l], ov_s,
                                     emat_s, gmat, diff_lam, gain, lam_init, n_sel_s)
        st_conv = jnp.transpose(state_conv[l], (1, 0, 2))
        st_pool = jnp.transpose(state_pool[l], (1, 0, 2))
        st_ffn = jnp.transpose(state_ffn[l], (1, 0, 2))
        x, zin = _merge_call(l, x, mod_s, g1, proj, st_conv, st_pool, o_a.reshape(1, bs, BRANCH_W),
                             o_d.reshape(1, bs, BRANCH_W), w_gate[l], wb[l], wo[l], pw[l], ps[l], conv_w[l], bs, False)
        x, up_new = _ffn_call(l, x, mod_s, g2, st_ffn, wu[l], ffn_conv[l], wd[l], fg, bs, False, l == depth - 1)
        kv_s.append(proj[0, :, P_KV:P_KV + KV_W].reshape(bs, 1, KV_W))
        win_new = proj[0, :, P_WIN:P_WIN + WIN_W].reshape(bs, 1, WIN_W)
        win_ext = jnp.concatenate([state_win_kv[l], win_new], axis=1)
        win_s.append(win_ext[:, win_ext.shape[1] - min(WINDOW, past + 1):])
        u_new = proj[0, :, P_UB:P_UB + POOL_W].reshape(bs, 1, POOL_W)
        pool_s.append(jnp.concatenate([state_pool[l], u_new], axis=1)[:, 1:])
        conv_s.append(jnp.concatenate([state_conv[l], zin.reshape(bs, 1, CONV_W)], axis=1)[:, 1:])
        ffn_s.append(jnp.concatenate([state_ffn[l], up_new.reshape(bs, 1, 2 * D_FF)], axis=1)[:, 1:])
    y_sample = x.reshape(bs, 1, D_MODEL)

    return (y_prompt, y_sample, jnp.stack(kv_p), jnp.stack(win_p), jnp.stack(pool_p), jnp.stack(conv_p),
            jnp.stack(ffn_p), jnp.stack(kv_s), jnp.stack(win_s), jnp.stack(pool_s), jnp.stack(conv_s),
            jnp.stack(ffn_s))
```

```python
import functools
import math

import numpy as np
import jax
import jax.numpy as jnp
from jax import lax
from jax.experimental import pallas as pl
from jax.experimental.pallas import tpu as pltpu

D_MODEL = 1024
NSA_HEADS = 4
NSA_HD = 64
CMP_STRIDE = 16
CMP_BLOCK = 32
CMP_HID = 128
SEL_BLOCK = 64
SEL_TOP_N = 16
WINDOW = 512
POOL_WINDOWS = (2, 4, 8, 16)
POOL_GROUP = 64
POOL_W = 256
POOL_KEEP = 15
CONV_W = 256
DIFF_HEADS = 4
DIFF_QK = 32
DIFF_V = 64
N_BRANCH = 4
BRANCH_W = 256
D_FF = 2816
QBLOCK = 128
PAGE_SIZE = 128
EPS = 1e-6
KV_W = 768
WIN_W = 128
GA_W = 12

P_KV, P_CV, P_QA, P_UB, P_QD, P_WIN, P_GA, P_W = 0, 768, 1536, 1792, 2048, 2304, 2432, 2560
KVB_W = KV_W + WIN_W

LANES = 128
SUBLANES = 8
VMEM_LIMIT = 56 * 1024 * 1024

NEG = -1e30
BIG = 1e30
LOG2E = math.log2(math.e)
BF = jnp.bfloat16
F32 = jnp.float32
HI = lax.Precision.HIGHEST

ROW_TILE = 512
KEY_CHUNK = 256
FF_CHUNK = 256
SAMPLE_GROUP = 2
AUX_TERMS = 4
AUX_SEL_ROW = 64
HALO = 16
FFN_HALO = 8

_NT = (((1,), (1,)), ((), ()))


def _dot(a, b, precision=None):
    return jnp.dot(a, b, preferred_element_type=F32, precision=precision)


def _dot_nt(a, b, precision=None):
    return lax.dot_general(a, b, _NT, preferred_element_type=F32, precision=precision)


def _slopes(n):
    return [float(2.0 ** (-8.0 * (k + 1) / n)) for k in range(n)]


def _norm_mod(x, g, scale, shift):
    ms = jnp.mean(x * x, axis=-1, keepdims=True)
    return (x * lax.rsqrt(ms + EPS) * g) * (1.0 + scale) + shift


def _cparams(sem):
    return pltpu.CompilerParams(dimension_semantics=sem, vmem_limit_bytes=VMEM_LIMIT)


def _const_spec(shape):
    nd = len(shape)
    return pl.BlockSpec(shape, lambda *a: (0,) * nd, pipeline_mode=pl.Buffered(1))


def _ada_kernel(c_ref, w_ref, b_ref, o_ref):
    c = c_ref[...]
    a = (c * jax.nn.sigmoid(c)).astype(BF)
    o_ref[0] = _dot(a, w_ref[0].astype(BF)) + b_ref[0]


def _ada_call(c_all, ada_w, ada_b):
    depth = ada_w.shape[0]
    nb = c_all.shape[0]
    return pl.pallas_call(
        _ada_kernel,
        out_shape=jax.ShapeDtypeStruct((depth, nb, 6 * D_MODEL), F32),
        grid=(depth, 6),
        in_specs=[
            pl.BlockSpec((nb, D_MODEL), lambda l, j: (0, 0)),
            pl.BlockSpec((1, D_MODEL, D_MODEL), lambda l, j: (l, 0, j)),
            pl.BlockSpec((1, 1, D_MODEL), lambda l, j: (l, 0, j)),
        ],
        out_specs=pl.BlockSpec((1, nb, D_MODEL), lambda l, j: (l, 0, j)),
        compiler_params=_cparams(("arbitrary", "arbitrary")),
    )(c_all, ada_w, ada_b.reshape(depth, 1, 6 * D_MODEL))


def _proj_kernel(x_ref, sc_ref, sh_ref, g_ref, w_ref, proj_ref, kvb_ref):
    h = _norm_mod(x_ref[0], g_ref[...], sc_ref[0, 0, 0], sh_ref[0, 0, 0]).astype(BF)
    y = _dot(h, w_ref[...])
    proj_ref[0] = y
    kvb_ref[0] = jnp.concatenate([y[:, P_KV:P_KV + KV_W], y[:, P_WIN:P_WIN + WIN_W]], axis=1).astype(BF)


def _mod_spec(l, which, rm):
    return pl.BlockSpec((1, 1, 1, rm, D_MODEL), lambda b, i: (l, b, which, 0, 0))


def _proj_call(l, x3, mod, g, w_packed, tm):
    bx, tx, _ = x3.shape
    rm = mod.shape[3]
    return pl.pallas_call(
        _proj_kernel,
        out_shape=(jax.ShapeDtypeStruct((bx, tx, P_W), F32), jax.ShapeDtypeStruct((bx, tx, KVB_W), BF)),
        grid=(bx, tx // tm),
        in_specs=[
            pl.BlockSpec((1, tm, D_MODEL), lambda b, i: (b, i, 0)),
            _mod_spec(l, 1, rm), _mod_spec(l, 0, rm),
            _const_spec((1, D_MODEL)),
            _const_spec((D_MODEL, P_W)),
        ],
        out_specs=(pl.BlockSpec((1, tm, P_W), lambda b, i: (b, i, 0)),
                   pl.BlockSpec((1, tm, KVB_W), lambda b, i: (b, i, 0))),
        compiler_params=_cparams(("arbitrary", "arbitrary")),
    )(x3, mod, mod, g, w_packed)


def _compress_finish(y, yb, w2):
    n = y.shape[0]
    yn = pltpu.roll(y, n - 1, axis=0)
    hid = CMP_HID
    hk = y[:, 0:hid] + yn[:, hid:2 * hid] + (yb[0:1, 0:hid] + yb[1:2, hid:2 * hid])
    hv = y[:, 2 * hid:3 * hid] + yn[:, 3 * hid:4 * hid] + (yb[0:1, 2 * hid:3 * hid] + yb[1:2, 3 * hid:4 * hid])
    act = jnp.concatenate([jax.nn.gelu(hk), jax.nn.gelu(hv)], axis=1).astype(BF)
    return _dot(act, w2)


def _compress_kernel(ch_ref, wc_ref, pe_ref, w2_ref, o_ref):
    y = _dot(ch_ref[0], wc_ref[...])
    yb = _dot(pe_ref[...], wc_ref[...])
    o_ref[0] = _compress_finish(y, yb, w2_ref[...]).astype(BF)


def _compress_call(chunks, wc, pe2, w2):
    b, n_ch, cw = chunks.shape
    return pl.pallas_call(
        _compress_kernel,
        out_shape=jax.ShapeDtypeStruct((b, n_ch, 2 * NSA_HD), BF),
        grid=(b,),
        in_specs=[pl.BlockSpec((1, n_ch, cw), lambda i: (i, 0, 0)),
                  _const_spec(wc.shape), _const_spec(pe2.shape), _const_spec(w2.shape)],
        out_specs=pl.BlockSpec((1, n_ch, 2 * NSA_HD), lambda i: (i, 0, 0)),
        compiler_params=_cparams(("arbitrary",)),
    )(chunks, wc, pe2, w2)


def _lam_value(lam_ref, lam_init):
    lp = lam_ref[0]
    a = jnp.sum(lp[0:1] * lp[1:2], axis=-1, keepdims=True)
    b = jnp.sum(lp[2:3] * lp[3:4], axis=-1, keepdims=True)
    return jnp.exp(a) - jnp.exp(b) + lam_init


def _group_rms(od, gmat, gain, lam_init):
    ms = _dot(od * od, gmat, precision=HI)
    return od * lax.rsqrt(ms + EPS) * gain * (1.0 - lam_init)


def _prompt_attn_kernel(qa_ref, qd_ref, ga_ref, kvb_ref, kvt_ref, kc_ref, kct_ref, ov_ref, aux_ref, qaux_ref,
                        gmat_ref, lam_ref, gain_ref, oa_ref, od_ref, m_sc, l_sc, acc_sc, s_sc, p_sc, a_sc, *,
                        lam_init, top):
    tq = QBLOCK
    kc_len = KEY_CHUNK
    i = pl.program_id(1)
    t0 = i * tq
    n_sel = ov_ref.shape[0]

    t_row = t0 + lax.broadcasted_iota(jnp.int32, (1, LANES), 1)
    feat = lax.broadcasted_iota(jnp.int32, (LANES, LANES), 0)

    qa = qa_ref[0] * (NSA_HD ** -0.5 * LOG2E)
    qa_t = [qa[:, 0:LANES].T, qa[:, LANES:2 * LANES].T]
    slabs = []
    for h in range(NSA_HEADS):
        tile = qa_t[h // 2]
        if h % 2:
            tile = pltpu.roll(tile, NSA_HD, axis=0)
        slabs.append(jnp.where(feat < NSA_HD, tile, 0.0))
    q_nsa = jnp.concatenate(slabs, axis=1).astype(BF)

    qd = qd_ref[0] * (DIFF_QK ** -0.5 * LOG2E)
    q_diff = []
    for g in range(2):
        tile = qd[:, g * LANES:(g + 1) * LANES].T
        slabs = [jnp.where(lax.shift_right_logical(feat, 5) == sl, tile, 0.0) for sl in range(4)]
        q_diff.append(jnp.concatenate(slabs, axis=1).astype(BF))

    def lanes_of(x, sl):
        return x[:, sl * tq:(sl + 1) * tq]

    kc = kc_ref[0]
    n_ch = kc.shape[0]
    s = _dot(kc, q_nsa)
    cmp_end = lax.broadcasted_iota(jnp.int32, (n_ch, LANES), 0) * CMP_STRIDE + (CMP_BLOCK - 1)
    vis = cmp_end <= t_row
    p_heads = []
    for h in range(NSA_HEADS):
        sh = jnp.where(vis, lanes_of(s, h), NEG)
        m = jnp.max(sh, axis=0, keepdims=True)
        e = jnp.where(vis, jnp.exp2(sh - m), 0.0)
        d = jnp.sum(e, axis=0, keepdims=True)
        p_heads.append(e / jnp.where(d > 0, d, 1.0))
    o_cmp = _dot(kct_ref[0], jnp.concatenate(p_heads, axis=1).astype(BF))
    p_sum = p_heads[0] + p_heads[1] + p_heads[2] + p_heads[3]
    imp_t = _dot(ov_ref[...], p_sum, precision=HI)

    j_full = lax.broadcasted_iota(jnp.int32, (n_sel, LANES), 0)
    valid = j_full * SEL_BLOCK <= t_row
    cur = lax.shift_right_logical(t_row, 6)
    forced = jnp.where(j_full == 0, 1, jnp.where(j_full == cur, 1, jnp.where(j_full == cur - 1, 1, 0)))
    score = jnp.where(valid, jnp.where(forced > 0, BIG, imp_t), -BIG)
    rank = jnp.zeros((n_sel, LANES), F32)
    for ii in range(n_sel):
        si = score[ii:ii + 1, :]
        rank = rank + jnp.where(si > score, 1.0, jnp.where(si == score, jnp.where(j_full > ii, 1.0, 0.0), 0.0))
    sel_neg = jnp.where(rank < top, 0.0, NEG)

    m_sc[...] = jnp.full(m_sc.shape, NEG, F32)
    l_sc[...] = jnp.zeros(l_sc.shape, F32)
    acc_sc[...] = jnp.zeros(acc_sc.shape, F32)

    aux_nsa = qaux_ref[0]
    pad = [jnp.zeros((AUX_SEL_ROW - n_sel, 4 * tq), F32)] if n_sel < AUX_SEL_ROW else []
    aux_sel = jnp.concatenate([aux_nsa[0:AUX_SEL_ROW], jnp.concatenate([sel_neg] * 4, axis=1)] + pad, axis=0)
    q_win = jnp.concatenate([q_nsa, aux_nsa.astype(BF)], axis=0)
    k_tile = [1, 2, 3]
    v_tile = [1, 4, 5]
    q_grp = [jnp.concatenate([q_nsa, aux_sel.astype(BF)], axis=0),
             jnp.concatenate([q_diff[0], qaux_ref[1].astype(BF)], axis=0),
             jnp.concatenate([q_diff[1], qaux_ref[2].astype(BF)], axis=0)]
    n_full = t0 // kc_len

    def scores(c, buf):
        ks = pl.multiple_of(c * kc_len, kc_len)
        aux_k = aux_ref[pl.ds(ks, kc_len), :]
        for grp in range(3):
            k_op = jnp.concatenate(
                [kvb_ref[0, pl.ds(ks, kc_len), k_tile[grp] * LANES:(k_tile[grp] + 1) * LANES], aux_k], axis=1)
            s_sc[buf, grp] = _dot(k_op, q_grp[grp])

    def softmax(c, buf, diagonal):
        if diagonal:
            pos = c * kc_len + lax.broadcasted_iota(jnp.int32, (kc_len, LANES), 0)
            causal_bias = jnp.where(pos <= t_row, 0.0, NEG)
        for grp in range(3):
            m_old_all = m_sc[grp]
            l_old_all = l_sc[grp]
            alphas, ms, ls = [], [], []
            for sl in range(4):
                s_h = s_sc[buf, grp, :, sl * tq:(sl + 1) * tq]
                if diagonal:
                    s_h = s_h + causal_bias
                m_old = lanes_of(m_old_all, sl)
                m_new = jnp.maximum(m_old, jnp.max(s_h, axis=0, keepdims=True))
                alpha = jnp.exp2(m_old - m_new)
                e = jnp.exp2(s_h - m_new)
                ls.append(alpha * lanes_of(l_old_all, sl) + jnp.sum(e, axis=0, keepdims=True))
                ms.append(m_new)
                alphas.append(alpha)
                p_sc[buf, grp, :, sl * tq:(sl + 1) * tq] = e.astype(BF)
            m_sc[grp] = jnp.concatenate(ms, axis=1)
            l_sc[grp] = jnp.concatenate(ls, axis=1)
            a_sc[buf, grp] = jnp.concatenate(alphas, axis=1)

    def values(c, buf):
        ks = pl.multiple_of(jnp.maximum(c, 0) * kc_len, kc_len)
        for grp in range(3):
            v_t = kvt_ref[0, v_tile[grp] * LANES:(v_tile[grp] + 1) * LANES, pl.ds(ks, kc_len)]
            acc_sc[grp] = a_sc[buf, grp] * acc_sc[grp] + _dot(v_t, p_sc[buf, grp])

    def stage(c, buf):
        scores(c + 1, 1 - buf)
        values(c - 1, 1 - buf)
        softmax(c, buf, False)

    def last(c, buf):
        values(c - 1, 1 - buf)
        softmax(c, buf, True)
        values(c, buf)

    scores(0, 0)
    p_sc[1] = jnp.zeros(p_sc.shape[1:], BF)
    a_sc[1] = jnp.ones(a_sc.shape[1:], F32)
    n_pairs = n_full // 2

    def pair(k, carry):
        stage(2 * k, 0)
        stage(2 * k + 1, 1)
        return carry

    lax.fori_loop(0, n_pairs, pair, 0)

    @pl.when(n_full == 2 * n_pairs)
    def _():
        last(n_full, 0)

    @pl.when(n_full != 2 * n_pairs)
    def _():
        stage(n_full - 1, 0)
        last(n_full, 1)

    o_sel = acc_sc[0] / l_sc[0]

    span = WINDOW + tq
    start = pl.multiple_of(jnp.maximum(i - WINDOW // tq, 0) * tq, tq)
    pos_w = start + lax.broadcasted_iota(jnp.int32, (span, LANES), 0)
    dist = t_row - pos_w
    mb_w = jnp.where(dist >= 0, jnp.where(dist < WINDOW, 0.0, NEG), NEG)
    k_win = jnp.concatenate([kvb_ref[0, pl.ds(start, span), KV_W:KV_W + WIN_W], aux_ref[pl.ds(start, span), :]],
                            axis=1)
    sw = _dot(k_win, q_win)
    pw = []
    for h in range(NSA_HEADS):
        s_h = lanes_of(sw, h) + mb_w
        e = jnp.exp2(s_h - jnp.max(s_h, axis=0, keepdims=True))
        pw.append((e / jnp.sum(e, axis=0, keepdims=True)).astype(BF))
    o_win = _dot(kvt_ref[0, KV_W:KV_W + WIN_W, pl.ds(start, span)], jnp.concatenate(pw, axis=1))

    ga_t = jax.nn.sigmoid(ga_ref[0]).T
    oa_h = []
    for h in range(NSA_HEADS):
        oa_h.append(ga_t[3 * h:3 * h + 1] * lanes_of(o_cmp, h) + ga_t[3 * h + 1:3 * h + 2] * lanes_of(o_sel, h)
                    + ga_t[3 * h + 2:3 * h + 3] * lanes_of(o_win, h))
    oa_ref[0] = jnp.concatenate(
        [jnp.concatenate([oa_h[0][NSA_HD:], oa_h[1][NSA_HD:]], axis=0).T,
         jnp.concatenate([oa_h[2][NSA_HD:], oa_h[3][NSA_HD:]], axis=0).T], axis=1)

    lam = _lam_value(lam_ref, lam_init)
    tiles = []
    for g in range(2):
        on = acc_sc[1 + g] / l_sc[1 + g]
        a0 = lanes_of(on, 0) - lam * lanes_of(on, 1)
        a1 = lanes_of(on, 2) - lam * lanes_of(on, 3)
        tiles.append(jnp.concatenate([a0[:DIFF_V], a1[DIFF_V:]], axis=0).T)
    od = jnp.concatenate(tiles, axis=1)
    od_ref[0] = _group_rms(od, gmat_ref[...], gain_ref[0], lam_init)


def _prompt_attn_call(l, proj, kvb, kvt, kcvc, kct, ov, aux, qaux, gmat, diff_lam, gain, lam_init):
    b, t, _ = proj.shape
    n_ch = kcvc.shape[1]
    n_sel = ov.shape[0]
    tq = QBLOCK
    assert n_sel <= LANES - AUX_SEL_ROW and KEY_CHUNK % tq == 0
    kern = functools.partial(_prompt_attn_kernel, lam_init=lam_init, top=min(SEL_TOP_N, n_sel))
    return pl.pallas_call(
        kern,
        out_shape=(jax.ShapeDtypeStruct((b, t, BRANCH_W), F32), jax.ShapeDtypeStruct((b, t, BRANCH_W), F32)),
        grid=(b, t // tq),
        in_specs=[
            pl.BlockSpec((1, tq, 256), lambda bi, i: (bi, i, P_QA // 256)),
            pl.BlockSpec((1, tq, 256), lambda bi, i: (bi, i, P_QD // 256)),
            pl.BlockSpec((1, tq, LANES), lambda bi, i: (bi, i, P_GA // LANES)),
            pl.BlockSpec((1, t, KVB_W), lambda bi, i: (bi, 0, 0)),
            pl.BlockSpec((1, KVB_W, t), lambda bi, i: (bi, 0, 0)),
            pl.BlockSpec((1, n_ch, 2 * NSA_HD), lambda bi, i: (bi, 0, 0)),
            pl.BlockSpec((1, 2 * NSA_HD, n_ch), lambda bi, i: (bi, 0, 0)),
            _const_spec(ov.shape), _const_spec(aux.shape), _const_spec(qaux.shape), _const_spec(gmat.shape),
            pl.BlockSpec((1, 4, DIFF_QK), lambda bi, i: (l, 0, 0)),
            pl.BlockSpec((1, 1, BRANCH_W), lambda bi, i: (l, 0, 0)),
        ],
        out_specs=(pl.BlockSpec((1, tq, BRANCH_W), lambda bi, i: (bi, i, 0)),
                   pl.BlockSpec((1, tq, BRANCH_W), lambda bi, i: (bi, i, 0))),
        scratch_shapes=[pltpu.VMEM((3, 1, 4 * tq), F32), pltpu.VMEM((3, 1, 4 * tq), F32),
                        pltpu.VMEM((3, LANES, 4 * tq), F32),
                        pltpu.VMEM((2, 3, KEY_CHUNK, 4 * tq), F32), pltpu.VMEM((2, 3, KEY_CHUNK, 4 * tq), BF),
                        pltpu.VMEM((2, 3, 1, 4 * tq), F32)],
        compiler_params=_cparams(("arbitrary", "arbitrary")),
    )(proj, proj, proj, kvb, kvt, kcvc, kct, ov, aux, qaux, gmat, diff_lam, gain)


def _sample_attn_seq(proj_ref, *refs, n_pages, lam_init, top, n_sel):
    pages = refs[:n_pages]
    (swin_ref, wc_ref, pe_ref, w2_ref, ov_ref, e_ref, gmat_ref, lam_ref, gain_ref, oa_ref, od_ref,
     cmp_rows) = refs[n_pages:]
    past = n_pages * PAGE_SIZE
    n_chp = past // CMP_STRIDE
    n_vis = (past - (CMP_BLOCK - 1)) // CMP_STRIDE + 1
    cur = past // SEL_BLOCK
    nsa_slopes = _slopes(NSA_HEADS)
    diff_slopes = _slopes(DIFF_HEADS)

    row = proj_ref[0]
    kv_new = row[:, P_KV:P_KV + KV_W]
    lane = lax.broadcasted_iota(jnp.int32, (1, LANES), 1)
    lo = lane < NSA_HD
    rid = lax.broadcasted_iota(jnp.int32, (SUBLANES, 1), 0)
    rid_full = lax.broadcasted_iota(jnp.int32, (SUBLANES, LANES), 0)

    def bcast(v):
        return jnp.broadcast_to(v, (SUBLANES, v.shape[1]))

    qa = row[:, P_QA:P_QA + 256] * (NSA_HD ** -0.5)
    b0, b1 = bcast(qa[:, 0:LANES]), bcast(qa[:, LANES:2 * LANES])
    cand = [b0, pltpu.roll(b0, NSA_HD, axis=1), b1, pltpu.roll(b1, NSA_HD, axis=1)]
    q8 = jnp.zeros((SUBLANES, LANES), F32)
    for h in range(NSA_HEADS):
        q8 = jnp.where(rid_full == h, cand[h], q8)
    q8 = jnp.where(lo, q8, 0.0)
    q8b = q8.astype(BF)
    slope_col = jnp.zeros((SUBLANES, 1), F32)
    for h in range(NSA_HEADS):
        slope_col = jnp.where(rid == h, nsa_slopes[h], slope_col)

    for j, pg in enumerate(pages):
        cmp_rows[j * PAGE_SIZE:(j + 1) * PAGE_SIZE, :] = pg[0, 0, :, 0:LANES]
    chunks = jnp.concatenate([cmp_rows[pl.ds(r, n_chp, stride=CMP_STRIDE), :].astype(BF)
                              for r in range(CMP_STRIDE)], axis=1)
    y = yield chunks
    yb = _dot(pe_ref[...], wc_ref[...])
    kc = _compress_finish(y, yb, w2_ref[...]).astype(BF)
    yield

    cl = lax.broadcasted_iota(jnp.int32, (1, n_chp), 1)
    vis = cl < n_vis
    s = jnp.where(vis, _dot_nt(q8b, kc), NEG)
    m = jnp.max(s, axis=-1, keepdims=True)
    e = jnp.where(vis, jnp.exp(s - m), 0.0)
    d = jnp.sum(e, axis=-1, keepdims=True)
    p_c = jnp.where(rid < NSA_HEADS, e / jnp.where(d > 0, d, 1.0), 0.0)
    o_cmp = _dot(p_c.astype(BF), kc)
    imp = jnp.sum(_dot(p_c, ov_ref[...], precision=HI), axis=0, keepdims=True)
    yield

    forced = jnp.where(lane == 0, 1, jnp.where(lane == cur, 1, jnp.where(lane == cur - 1, 1, 0)))
    score = jnp.where(lane < n_sel, jnp.where(forced > 0, BIG, imp), -BIG)
    r_row = jnp.broadcast_to(score, (LANES, LANES))
    r_col = r_row.T
    ii = lax.broadcasted_iota(jnp.int32, (LANES, LANES), 0)
    jj = lax.broadcasted_iota(jnp.int32, (LANES, LANES), 1)
    inc = jnp.where(r_col > r_row, 1.0, jnp.where(r_col == r_row, jnp.where(ii < jj, 1.0, 0.0), 0.0))
    rank = jnp.sum(inc, axis=0, keepdims=True)
    sel = jnp.where(rank < top, 1.0, 0.0)
    sel_exp = _dot(bcast(sel).astype(BF), e_ref[...])
    yield

    pos = lax.broadcasted_iota(jnp.int32, (1, past), 1)
    rel = (pos - past).astype(F32)

    def attend_new(q, sc, k_tile, kv_tile_new):
        s_new = jnp.sum(q * kv_tile_new, axis=-1, keepdims=True)
        mm = jnp.maximum(jnp.max(sc, axis=-1, keepdims=True), s_new)
        ee = jnp.exp(sc - mm)
        e_new = jnp.exp(s_new - mm)
        den = jnp.sum(ee, axis=-1, keepdims=True) + e_new
        return (_dot(ee.astype(BF), k_tile) + e_new * kv_tile_new) / den

    k_sel = jnp.concatenate([pg[0, 0, :, LANES:2 * LANES] for pg in pages], axis=0).astype(BF)
    sc = _dot_nt(q8b, k_sel) + slope_col * rel + jnp.where(sel_exp > 0.5, 0.0, NEG)
    o_sel = attend_new(q8, sc, k_sel, kv_new[:, LANES:2 * LANES])
    yield

    k_win = swin_ref[0, 0].astype(BF)
    nw = k_win.shape[0]
    wl = lax.broadcasted_iota(jnp.int32, (1, nw), 1)
    dist_w = nw - wl
    sc = _dot_nt(q8b, k_win) - slope_col * dist_w.astype(F32) + jnp.where(dist_w < WINDOW, 0.0, NEG)
    o_win = attend_new(q8, sc, k_win, row[:, P_WIN:P_WIN + WIN_W])

    ga8 = bcast(jax.nn.sigmoid(row[:, P_GA:P_GA + LANES]))
    gates = [jnp.sum(jnp.where(lane == 3 * rid_full + k, ga8, 0.0), axis=-1, keepdims=True) for k in range(3)]
    oa8 = gates[0] * o_cmp + gates[1] * o_sel + gates[2] * o_win
    oa_ref[0] = jnp.concatenate(
        [jnp.where(lo, pltpu.roll(oa8[0:1], NSA_HD, axis=1), oa8[1:2]),
         jnp.where(lo, pltpu.roll(oa8[2:3], NSA_HD, axis=1), oa8[3:4])], axis=1)
    yield

    qd = bcast(row[:, P_QD:P_QD + 256] * (DIFF_QK ** -0.5))
    lane2 = lax.broadcasted_iota(jnp.int32, (SUBLANES, 2 * LANES), 1)
    rid2 = lax.broadcasted_iota(jnp.int32, (SUBLANES, 2 * LANES), 0)
    qd8 = jnp.where(lax.shift_right_logical(lane2, 5) == rid2, qd, 0.0)
    dslope = jnp.zeros((SUBLANES, 1), F32)
    for r in range(SUBLANES):
        dslope = jnp.where(rid == r, diff_slopes[r // 2], dslope)
    k_d = jnp.concatenate([pg[0, 0, :, 2 * LANES:4 * LANES] for pg in pages], axis=0).astype(BF)
    v_d = jnp.concatenate([pg[0, 0, :, 4 * LANES:6 * LANES] for pg in pages], axis=0).astype(BF)
    sc = _dot_nt(qd8.astype(BF), k_d) + dslope * rel
    s_new = jnp.sum(qd8 * kv_new[:, 2 * LANES:4 * LANES], axis=-1, keepdims=True)
    mm = jnp.maximum(jnp.max(sc, axis=-1, keepdims=True), s_new)
    ee = jnp.exp(sc - mm)
    e_new = jnp.exp(s_new - mm)
    den = jnp.sum(ee, axis=-1, keepdims=True) + e_new
    o8 = (_dot(ee.astype(BF), v_d) + e_new * kv_new[:, 4 * LANES:6 * LANES]) / den
    lam = _lam_value(lam_ref, lam_init)
    coef = jnp.where((rid2 & 1) == 0, 1.0, -lam)
    head_mask = lax.shift_right_logical(lane2, 6) == lax.shift_right_logical(rid2, 1)
    od = jnp.sum(jnp.where(head_mask, coef * o8, 0.0), axis=0, keepdims=True)
    od8 = _group_rms(bcast(od), gmat_ref[...], gain_ref[0], lam_init)
    od_ref[0] = od8[0:1]


def _sample_group_kernel(pt_ref, proj_ref, *refs, n_pages, group, **kw):
    pages = refs[:group * n_pages]
    swin_ref, *consts, oa_ref, od_ref, cmp_rows = refs[group * n_pages:]
    seqs = []
    for s in range(group):
        one = pl.ds(s, 1)
        seqs.append(_sample_attn_seq(proj_ref.at[one], *pages[s * n_pages:(s + 1) * n_pages], swin_ref.at[:, one],
                                     *consts, oa_ref.at[one], od_ref.at[one], cmp_rows.at[s], n_pages=n_pages, **kw))
    chunks = [next(seq) for seq in seqs]
    n_chp = chunks[0].shape[0]
    y_all = _dot(jnp.concatenate(chunks, axis=0), consts[0][...])
    for s, seq in enumerate(seqs):
        seq.send(y_all[s * n_chp:(s + 1) * n_chp])
    live = True
    while live:
        live = False
        for seq in seqs:
            live = next(seq, "done") != "done" or live


def _sample_attn_call(l, page_table, proj, cache_kv, state_win, wc, pe2, w2, ov, emat, gmat, diff_lam, gain,
                      lam_init, n_sel):
    bs = proj.shape[0]
    n_pages = page_table.shape[1]
    nw = state_win.shape[2]
    group = SAMPLE_GROUP if bs % SAMPLE_GROUP == 0 else 1
    kern = functools.partial(_sample_group_kernel, n_pages=n_pages, group=group, lam_init=lam_init,
                             top=min(SEL_TOP_N, n_sel), n_sel=n_sel)

    def page_spec(s, j):
        return pl.BlockSpec((1, 1, PAGE_SIZE, KV_W), lambda b, pt: (l, pt[group * b + s, j], 0, 0))

    def cspec(shape):
        nd = len(shape)
        return pl.BlockSpec(shape, lambda b, pt: (0,) * nd, pipeline_mode=pl.Buffered(1))

    grid_spec = pltpu.PrefetchScalarGridSpec(
        num_scalar_prefetch=1,
        grid=(bs // group,),
        in_specs=[pl.BlockSpec((group, 1, P_W), lambda b, pt: (b, 0, 0))]
        + [page_spec(s, j) for s in range(group) for j in range(n_pages)]
        + [pl.BlockSpec((1, group, nw, WIN_W), lambda b, pt: (l, b, 0, 0)),
           cspec(wc.shape), cspec(pe2.shape), cspec(w2.shape), cspec(ov.shape), cspec(emat.shape),
           cspec(gmat.shape),
           pl.BlockSpec((1, 4, DIFF_QK), lambda b, pt: (l, 0, 0)),
           pl.BlockSpec((1, 1, BRANCH_W), lambda b, pt: (l, 0, 0))],
        out_specs=(pl.BlockSpec((group, 1, BRANCH_W), lambda b, pt: (b, 0, 0)),
                   pl.BlockSpec((group, 1, BRANCH_W), lambda b, pt: (b, 0, 0))),
        scratch_shapes=[pltpu.VMEM((group, n_pages * PAGE_SIZE, LANES), F32)],
    )
    return pl.pallas_call(
        kern,
        out_shape=(jax.ShapeDtypeStruct((bs, 1, BRANCH_W), F32), jax.ShapeDtypeStruct((bs, 1, BRANCH_W), F32)),
        grid_spec=grid_spec,
        compiler_params=_cparams(("arbitrary",)),
    )(page_table, proj, *([cache_kv] * (group * n_pages)), state_win, wc, pe2, w2, ov, emat, gmat, diff_lam, gain)


def _merge_kernel(*refs, halo):
    if halo:
        (x_ref, sc_ref, sh_ref, gt_ref, g_ref, cv_ref, ub_ref, cvp_ref, ubp_ref, oa_ref, od_ref,
         wg_ref, wb_ref, wo_ref, pw_ref, ps_ref, cw_ref, xo_ref, zin_ref) = refs
    else:
        (x_ref, sc_ref, sh_ref, gt_ref, g_ref, cv_ref, ub_ref, stc_ref, stp_ref, oa_ref, od_ref,
         wg_ref, wb_ref, wo_ref, pw_ref, ps_ref, cw_ref, xo_ref, zin_ref) = refs
    i = pl.program_id(1)
    x = x_ref[0]
    tm = x.shape[0]
    h = _norm_mod(x, g_ref[...], sc_ref[0, 0, 0], sh_ref[0, 0, 0]).astype(BF)

    cv = cv_ref[0]
    u = ub_ref[0]
    bg = cv[:, 0:CONV_W]
    zin = cv[:, CONV_W:2 * CONV_W] * cv[:, 2 * CONV_W:3 * CONV_W]
    zin_ref[0] = zin
    cw = cw_ref[...]
    lane = lax.broadcasted_iota(jnp.int32, (1, POOL_W), 1)
    win_lane = jnp.where(lane < 64, 2, jnp.where(lane < 128, 4, jnp.where(lane < 192, 8, 16)))

    if halo:
        keep = jnp.where(i > 0, 1.0, 0.0)
        cvp = cvp_ref[0] * keep
        ubp = ubp_ref[0] * keep
        z_ext = jnp.concatenate([cvp[:, CONV_W:2 * CONV_W] * cvp[:, 2 * CONV_W:3 * CONV_W], zin], axis=0)
        z = (cw[0:1] * pltpu.roll(z_ext, 2, axis=0) + cw[1:2] * pltpu.roll(z_ext, 1, axis=0)
             + cw[2:3] * z_ext)[HALO:]
        ext = jnp.concatenate([ubp, u], axis=0)
        s2 = ext + pltpu.roll(ext, 1, axis=0)
        s4 = s2 + pltpu.roll(s2, 2, axis=0)
        s8 = s4 + pltpu.roll(s4, 4, axis=0)
        s16 = s8 + pltpu.roll(s8, 8, axis=0)
        tot = jnp.where(lane < 64, s2, jnp.where(lane < 128, s4, jnp.where(lane < 192, s8, s16)))[HALO:]
        pos1 = i * tm + lax.broadcasted_iota(jnp.int32, (tm, 1), 0) + 1
        cnt = jnp.minimum(win_lane, pos1).astype(F32)
    else:
        z = cw[0:1] * stc_ref[0] + cw[1:2] * stc_ref[1] + cw[2:3] * zin
        s2 = u + stp_ref[POOL_KEEP - 1]
        s4 = s2 + stp_ref[POOL_KEEP - 2] + stp_ref[POOL_KEEP - 3]
        s8 = s4
        for k in range(4, 8):
            s8 = s8 + stp_ref[POOL_KEEP - k]
        s16 = s8
        for k in range(8, 16):
            s16 = s16 + stp_ref[POOL_KEEP - k]
        tot = jnp.where(lane < 64, s2, jnp.where(lane < 128, s4, jnp.where(lane < 192, s8, s16)))
        cnt = win_lane.astype(F32)
    o_c = bg * z
    pooled = tot / cnt - u
    o_b = _dot(pooled.astype(BF), pw_ref[...]) * ps_ref[...]

    branches = [oa_ref[0], o_b, o_c, od_ref[0]]
    mixed = jnp.zeros((tm, D_MODEL), F32)
    for n in range(N_BRANCH):
        pb = _dot(branches[n].astype(BF), wb_ref[n])
        gate = jax.nn.sigmoid(_dot(h, wg_ref[:, n * D_MODEL:(n + 1) * D_MODEL]))
        mixed = mixed + gate * pb
    xo_ref[0] = x + gt_ref[0, 0, 0] * _dot(mixed.astype(BF), wo_ref[...])


def _merge_call(l, x3, mod, g, proj, prev_c, prev_p, o_a, o_d, wg, wb, wo, pw, ps, cw, tm, halo):
    bx, tx, _ = x3.shape
    rm = mod.shape[3]
    if halo:
        nblk = tm // HALO
        prev_specs = [
            pl.BlockSpec((1, HALO, 3 * CONV_W), lambda b, i: (b, jnp.maximum(i * nblk - 1, 0), P_CV // (3 * CONV_W))),
            pl.BlockSpec((1, HALO, POOL_W), lambda b, i: (b, jnp.maximum(i * nblk - 1, 0), P_UB // POOL_W)),
        ]
    else:
        prev_specs = [pl.BlockSpec(prev_c.shape, lambda b, i: (0, 0, 0)),
                      pl.BlockSpec(prev_p.shape, lambda b, i: (0, 0, 0))]
    return pl.pallas_call(
        functools.partial(_merge_kernel, halo=halo),
        out_shape=(jax.ShapeDtypeStruct((bx, tx, D_MODEL), F32), jax.ShapeDtypeStruct((bx, tx, CONV_W), F32)),
        grid=(bx, tx // tm),
        in_specs=[
            pl.BlockSpec((1, tm, D_MODEL), lambda b, i: (b, i, 0)),
            _mod_spec(l, 1, rm), _mod_spec(l, 0, rm), _mod_spec(l, 2, rm),
            _const_spec((1, D_MODEL)),
            pl.BlockSpec((1, tm, 3 * CONV_W), lambda b, i: (b, i, P_CV // (3 * CONV_W))),
            pl.BlockSpec((1, tm, POOL_W), lambda b, i: (b, i, P_UB // POOL_W)),
            *prev_specs,
            pl.BlockSpec((1, tm, BRANCH_W), lambda b, i: (b, i, 0)),
            pl.BlockSpec((1, tm, BRANCH_W), lambda b, i: (b, i, 0)),
            _const_spec(wg.shape), _const_spec(wb.shape), _const_spec(wo.shape),
            _const_spec(pw.shape), _const_spec(ps.shape), _const_spec(cw.shape),
        ],
        out_specs=(pl.BlockSpec((1, tm, D_MODEL), lambda b, i: (b, i, 0)),
                   pl.BlockSpec((1, tm, CONV_W), lambda b, i: (b, i, 0))),
        compiler_params=_cparams(("arbitrary", "arbitrary")),
    )(x3, mod, mod, mod, g, proj, proj, prev_c, prev_p, o_a, o_d, wg, wb, wo, pw, ps, cw)


def _ffn_kernel(*refs, halo, final, keep_rows):
    (x_ref, sc_ref, sh_ref, gt_ref, g_ref, prev_ref, wu_ref, fc_ref, wd_ref, fg_ref, xo_ref, up_ref) = refs
    i = pl.program_id(1)
    x = x_ref[0]
    tm = x.shape[0]
    g = g_ref[...]
    sc = sc_ref[0, 0, 0]
    sh = sh_ref[0, 0, 0]
    if halo:
        hx = jnp.concatenate([prev_ref[0], x], axis=0)
        h2 = _norm_mod(hx, g, sc, sh).astype(BF)
        rows = lax.broadcasted_iota(jnp.int32, (tm + FFN_HALO, 1), 0)
        live = jnp.where(rows >= FFN_HALO, 1.0, jnp.where(i > 0, 1.0, 0.0))
    else:
        h2 = _norm_mod(x, g, sc, sh).astype(BF)
    acc = jnp.zeros((tm, D_MODEL), F32)
    for c in range(D_FF // FF_CHUNK):
        halves = []
        for part in range(2):
            col = part * D_FF + c * FF_CHUNK
            up = _dot(h2, wu_ref[:, col:col + FF_CHUNK])
            w = fc_ref[:, col:col + FF_CHUNK]
            if halo:
                up = up * live
                up_ref[0, :, col:col + FF_CHUNK] = up[FFN_HALO + tm - keep_rows:]
                upc = (w[0:1] * pltpu.roll(up, 2, axis=0) + w[1:2] * pltpu.roll(up, 1, axis=0)
                       + w[2:3] * up)[FFN_HALO:]
            else:
                up_ref[0, :, col:col + FF_CHUNK] = up
                upc = (w[0:1] * prev_ref[0, :, col:col + FF_CHUNK] + w[1:2] * prev_ref[1, :, col:col + FF_CHUNK]
                       + w[2:3] * up)
            halves.append(upc)
        val, gg = halves
        act = (gg * jax.nn.sigmoid(gg) * val).astype(BF)
        acc = acc + _dot(act, wd_ref[c * FF_CHUNK:(c + 1) * FF_CHUNK, :])
    out = x + gt_ref[0, 0, 0] * acc
    if final:
        ms = jnp.mean(out * out, axis=-1, keepdims=True)
        out = out * lax.rsqrt(ms + EPS) * fg_ref[...]
    xo_ref[0] = out


def _ffn_call(l, x3, mod, g, prev, wu, fc, wd, fg, tm, halo, final):
    bx, tx, _ = x3.shape
    rm = mod.shape[3]
    keep_rows = SUBLANES if halo else tm
    if halo:
        nblk = tm // FFN_HALO
        prev_spec = pl.BlockSpec((1, FFN_HALO, D_MODEL), lambda b, i: (b, jnp.maximum(i * nblk - 1, 0), 0))
    else:
        prev_spec = pl.BlockSpec(prev.shape, lambda b, i: (0, 0, 0))
    return pl.pallas_call(
        functools.partial(_ffn_kernel, halo=halo, final=final, keep_rows=keep_rows),
        out_shape=(jax.ShapeDtypeStruct((bx, tx, D_MODEL), F32),
                   jax.ShapeDtypeStruct((bx, keep_rows, 2 * D_FF), F32)),
        grid=(bx, tx // tm),
        in_specs=[
            pl.BlockSpec((1, tm, D_MODEL), lambda b, i: (b, i, 0)),
            _mod_spec(l, 4, rm), _mod_spec(l, 3, rm), _mod_spec(l, 5, rm),
            _const_spec((1, D_MODEL)),
            prev_spec,
            _const_spec(wu.shape), _const_spec(fc.shape), _const_spec(wd.shape), _const_spec((1, D_MODEL)),
        ],
        out_specs=(pl.BlockSpec((1, tm, D_MODEL), lambda b, i: (b, i, 0)),
                   pl.BlockSpec((1, keep_rows, 2 * D_FF), lambda b, i: (b, 0, 0))),
        compiler_params=_cparams(("arbitrary", "arbitrary")),
    )(x3, mod, mod, mod, g, prev, wu, fc, wd, fg)


def _pack_w_in(w_in):
    o = np.cumsum([0, 256, KV_W, WIN_W, GA_W, POOL_W, 3 * CONV_W, 256, N_BRANCH * D_MODEL])
    q_a, kv, win, g_a, u_b, cv, q_d, gates = [w_in[..., o[k]:o[k + 1]] for k in range(8)]
    g_a = jnp.pad(g_a, ((0, 0), (0, 0), (0, LANES - GA_W)))
    packed = jnp.concatenate([kv, cv, q_a, u_b, q_d, win, g_a], axis=-1).astype(BF)
    return packed, gates.astype(BF)


def _compress_weights(cmp_pe, cmp_w1, cmp_w2):
    depth = cmp_w1.shape[0]
    half = CMP_STRIDE * NSA_HD
    w1 = cmp_w1.reshape(depth, 2, 2, CMP_STRIDE, NSA_HD, CMP_HID)
    z = jnp.zeros((depth, CMP_STRIDE, NSA_HD, CMP_HID), cmp_w1.dtype)
    cols = []
    for kv in range(2):
        for hf in range(2):
            blk = w1[:, kv, hf]
            pair = (blk, z) if kv == 0 else (z, blk)
            cols.append(jnp.concatenate(pair, axis=2).reshape(depth, 2 * half, CMP_HID))
    wc = jnp.concatenate(cols, axis=-1).astype(BF)
    pe = cmp_pe.reshape(depth, 2, 2, CMP_STRIDE, NSA_HD)
    pe2 = jnp.concatenate([pe[:, 0], pe[:, 1]], axis=-1).reshape(depth, 2, 2 * half)
    pe2 = jnp.pad(pe2, ((0, 0), (0, SUBLANES - 2), (0, 0))).astype(BF)
    zz = jnp.zeros((depth, CMP_HID, NSA_HD), cmp_w2.dtype)
    w2 = jnp.concatenate([jnp.concatenate([cmp_w2[:, 0], zz], axis=-1),
                          jnp.concatenate([zz, cmp_w2[:, 1]], axis=-1)], axis=1).astype(BF)
    return wc, pe2, w2


def _overlap(n_cmp, n_sel):
    cs = np.arange(n_cmp)[:, None] * CMP_STRIDE
    ss = np.arange(n_sel)[None, :] * SEL_BLOCK
    ov = np.minimum(cs + CMP_BLOCK, ss + SEL_BLOCK) - np.maximum(cs, ss)
    return (np.maximum(ov, 0) / CMP_BLOCK).astype(np.float32)


def _expand_matrix(n_rows, n_pos):
    return (np.arange(n_pos)[None, :] // SEL_BLOCK == np.arange(n_rows)[:, None]).astype(np.float32)


def _bf16_terms(x, n):
    terms = []
    for _ in range(n):
        u = np.array(x, np.float32).view(np.uint32)
        t = ((u + ((u >> 16) & 1) + 0x7FFF) & np.uint32(0xFFFF0000)).view(np.float32)
        terms.append(float(t))
        x = x - float(t)
    return terms


def _aux_key_table(n_pos):
    pos = np.arange(n_pos)
    tab = np.zeros((n_pos, LANES), np.float32)
    tab[:, 0:AUX_TERMS] = (pos // SEL_BLOCK)[:, None]
    tab[:, AUX_TERMS:2 * AUX_TERMS] = (pos % SEL_BLOCK)[:, None]
    tab[pos, AUX_SEL_ROW + pos // SEL_BLOCK] = 1.0
    return tab


def _aux_query_rows():
    terms = _bf16_terms(LOG2E, AUX_TERMS)
    slopes = _slopes(NSA_HEADS)
    slab_slope = [[slopes[h] for h in range(4)], [slopes[sl // 2] for sl in range(4)],
                  [slopes[2 + sl // 2] for sl in range(4)]]
    out = np.zeros((3, LANES, 4 * QBLOCK), np.float32)
    for g in range(3):
        for sl in range(4):
            for k, c in enumerate(terms):
                out[g, k, sl * QBLOCK:(sl + 1) * QBLOCK] = SEL_BLOCK * slab_slope[g][sl] * c
                out[g, AUX_TERMS + k, sl * QBLOCK:(sl + 1) * QBLOCK] = slab_slope[g][sl] * c
    return out


def _group_mean_matrix():
    idx = np.arange(DIFF_HEADS * DIFF_V) // DIFF_V
    return (idx[:, None] == idx[None, :]).astype(np.float32) / DIFF_V


def _pool_blockdiag(pool_w):
    depth, n, g, _ = pool_w.shape
    out = jnp.zeros((depth, n * g, n * g), pool_w.dtype)
    for k in range(n):
        out = out.at[:, k * g:(k + 1) * g, k * g:(k + 1) * g].set(pool_w[:, k])
    return out.astype(BF)


def kernel(x_prompt, x_sample, cache_kv, state_win_kv, state_pool, state_conv, state_ffn, page_table, c_prompt, c_sample, norm_g, ada_w, ada_b, w_in, cmp_pe, cmp_w1, cmp_w2, diff_lam, diff_norm_g, pool_w, pool_scale, conv_w, w_branch, w_out, w_up, ffn_conv, w_down, final_g):
    depth = w_in.shape[0]
    bp, seq, _ = x_prompt.shape
    bs = x_sample.shape[0]
    n_pages = page_table.shape[1]
    past = n_pages * PAGE_SIZE
    tm = min(ROW_TILE, seq)
    assert seq % tm == 0 and seq % QBLOCK == 0 and seq >= WINDOW + QBLOCK and x_sample.shape[1] == 1

    w_packed, w_gate = _pack_w_in(w_in)
    wc, pe2, w2 = _compress_weights(cmp_pe, cmp_w1, cmp_w2)
    wb = w_branch.astype(BF)
    wo = w_out.astype(BF)
    wu = w_up.astype(BF)
    wd = w_down.astype(BF)
    pw = _pool_blockdiag(pool_w)
    ps = pool_scale.reshape(depth, 1, POOL_W)
    gain = jnp.tile(diff_norm_g, (1, DIFF_HEADS)).reshape(depth, 1, DIFF_HEADS * DIFF_V)
    fg = final_g.reshape(1, D_MODEL)
    gmat = jnp.asarray(_group_mean_matrix())

    n_ch = seq // CMP_STRIDE
    n_sel_p = seq // SEL_BLOCK
    ov_p = np.zeros((n_sel_p, n_ch), np.float32)
    ov_p[:, :n_ch - 1] = _overlap(n_ch - 1, n_sel_p).T
    ov_p = jnp.asarray(ov_p)
    aux_p = jnp.asarray(_aux_key_table(seq), dtype=BF)
    qaux = jnp.asarray(_aux_query_rows())
    n_sel_s = -(-(past + 1) // SEL_BLOCK)
    n_chp = past // CMP_STRIDE
    ov_s = np.zeros((n_chp, LANES), np.float32)
    ov_s[:, :n_sel_s] = _overlap(n_chp, n_sel_s)
    ov_s = jnp.asarray(ov_s)
    emat_s = jnp.asarray(_expand_matrix(LANES, past), dtype=BF)

    mod = _ada_call(jnp.concatenate([c_prompt, c_sample], axis=0), ada_w, ada_b)
    mod_p = mod[:, :bp].reshape(depth, bp, 6, 1, D_MODEL)
    mod_s = jnp.transpose(mod[:, bp:].reshape(depth, bs, 6, D_MODEL), (0, 2, 1, 3)).reshape(depth, 1, 6, bs, D_MODEL)

    x = x_prompt
    kv_p, win_p, pool_p, conv_p, ffn_p = [], [], [], [], []
    for l in range(depth):
        lam_init = 0.8 - 0.6 * math.exp(-0.3 * l)
        g1 = norm_g[l, 0].reshape(1, D_MODEL)
        g2 = norm_g[l, 1].reshape(1, D_MODEL)
        proj, kvb = _proj_call(l, x, mod_p, g1, w_packed[l], tm)
        chunks = kvb[:, :, 0:2 * NSA_HD].reshape(bp, n_ch, CMP_STRIDE * 2 * NSA_HD)
        kcvc = _compress_call(chunks, wc[l], pe2[l], w2[l])
        o_a, o_d = _prompt_attn_call(l, proj, kvb, jnp.swapaxes(kvb, 1, 2), kcvc, jnp.swapaxes(kcvc, 1, 2),
                                     ov_p, aux_p, qaux, gmat, diff_lam, gain, lam_init)
        x, zin = _merge_call(l, x, mod_p, g1, proj, proj, proj, o_a, o_d, w_gate[l], wb[l], wo[l], pw[l], ps[l],
                             conv_w[l], tm, True)
        x, up_last = _ffn_call(l, x, mod_p, g2, x, wu[l], ffn_conv[l], wd[l], fg, tm, True, l == depth - 1)
        kv_p.append(proj[:, :, P_KV:P_KV + KV_W])
        win_p.append(proj[:, seq - min(WINDOW, seq):, P_WIN:P_WIN + WIN_W])
        pool_p.append(proj[:, seq - POOL_KEEP:, P_UB:P_UB + POOL_W])
        conv_p.append(zin[:, seq - 2:])
        ffn_p.append(up_last[:, SUBLANES - 2:])
    y_prompt = x

    x = x_sample.reshape(1, bs, D_MODEL)
    kv_s, win_s, pool_s, conv_s, ffn_s = [], [], [], [], []
    for l in range(depth):
        lam_init = 0.8 - 0.6 * math.exp(-0.3 * l)
        g1 = norm_g[l, 0].reshape(1, D_MODEL)
        g2 = norm_g[l, 1].reshape(1, D_MODEL)
        proj, _ = _proj_call(l, x, mod_s, g1, w_packed[l], bs)
        proj_rows = proj.reshape(bs, 1, P_W)
        o_a, o_d = _sample_attn_call(l, page_table, proj_rows, cache_kv, state_win_kv, wc[l], pe2[l], w2[l], ov_s,
                                     emat_s, gmat, diff_lam, gain, lam_init, n_sel_s)
        st_conv = jnp.transpose(state_conv[l], (1, 0, 2))
        st_pool = jnp.transpose(state_pool[l], (1, 0, 2))
        st_ffn = jnp.transpose(state_ffn[l], (1, 0, 2))
        x, zin = _merge_call(l, x, mod_s, g1, proj, st_conv, st_pool, o_a.reshape(1, bs, BRANCH_W),
                             o_d.reshape(1, bs, BRANCH_W), w_gate[l], wb[l], wo[l], pw[l], ps[l], conv_w[l], bs, False)
        x, up_new = _ffn_call(l, x, mod_s, g2, st_ffn, wu[l], ffn_conv[l], wd[l], fg, bs, False, l == depth - 1)
        kv_s.append(proj[0, :, P_KV:P_KV + KV_W].reshape(bs, 1, KV_W))
        win_new = proj[0, :, P_WIN:P_WIN + WIN_W].reshape(bs, 1, WIN_W)
        win_ext = jnp.concatenate([state_win_kv[l], win_new], axis=1)
        win_s.append(win_ext[:, win_ext.shape[1] - min(WINDOW, past + 1):])
        u_new = proj[0, :, P_UB:P_UB + POOL_W].reshape(bs, 1, POOL_W)
        pool_s.append(jnp.concatenate([state_pool[l], u_new], axis=1)[:, 1:])
        conv_s.append(jnp.concatenate([state_conv[l], zin.reshape(bs, 1, CONV_W)], axis=1)[:, 1:])
        ffn_s.append(jnp.concatenate([state_ffn[l], up_new.reshape(bs, 1, 2 * D_FF)], axis=1)[:, 1:])
    y_sample = x.reshape(bs, 1, D_MODEL)

    return (y_prompt, y_sample, jnp.stack(kv_p), jnp.stack(win_p), jnp.stack(pool_p), jnp.stack(conv_p),
            jnp.stack(ffn_p), jnp.stack(kv_s), jnp.stack(win_s), jnp.stack(pool_s), jnp.stack(conv_s),
            jnp.stack(ffn_s))
```

```python
import functools
import math

import numpy as np
import jax
import jax.numpy as jnp
from jax import lax
from jax.experimental import pallas as pl
from jax.experimental.pallas import tpu as pltpu

D_MODEL = 1024
NSA_HEADS = 4
NSA_HD = 64
CMP_STRIDE = 16
CMP_BLOCK = 32
CMP_HID = 128
SEL_BLOCK = 64
SEL_TOP_N = 16
WINDOW = 512
POOL_WINDOWS = (2, 4, 8, 16)
POOL_GROUP = 64
POOL_W = 256
POOL_KEEP = 15
CONV_W = 256
DIFF_HEADS = 4
DIFF_QK = 32
DIFF_V = 64
N_BRANCH = 4
BRANCH_W = 256
D_FF = 2816
QBLOCK = 128
PAGE_SIZE = 128
EPS = 1e-6
KV_W = 768
WIN_W = 128
GA_W = 12

P_KV, P_CV, P_QA, P_UB, P_QD, P_WIN, P_GA, P_W = 0, 768, 1536, 1792, 2048, 2304, 2432, 2560
KVB_W = KV_W + WIN_W
R_CV, R_QA, R_UB, R_QD, R_WIN, R_GA, R_W = [p - KV_W for p in (P_CV, P_QA, P_UB, P_QD, P_WIN, P_GA, P_W)]

LANES = 128
SUBLANES = 8
VMEM_LIMIT = 56 * 1024 * 1024

NEG = -1e30
BIG = 1e30
LOG2E = math.log2(math.e)
BF = jnp.bfloat16
F32 = jnp.float32
HI = lax.Precision.HIGHEST

ROW_TILE = 512
KEY_CHUNK = 256
FF_CHUNK = 2816
SAMPLE_GROUP = 2
AUX_TERMS = 4
AUX_SEL_ROW = 64
HALO = 16
FFN_HALO = 8

_NT = (((1,), (1,)), ((), ()))


def _dot(a, b, precision=None):
    return jnp.dot(a, b, preferred_element_type=F32, precision=precision)


def _dot_nt(a, b, precision=None):
    return lax.dot_general(a, b, _NT, preferred_element_type=F32, precision=precision)


def _slopes(n):
    return [float(2.0 ** (-8.0 * (k + 1) / n)) for k in range(n)]


def _norm_mod(x, g, scale, shift):
    ms = jnp.mean(x * x, axis=-1, keepdims=True)
    return (x * lax.rsqrt(ms + EPS) * g) * (1.0 + scale) + shift


def _cparams(sem):
    return pltpu.CompilerParams(dimension_semantics=sem, vmem_limit_bytes=VMEM_LIMIT)


def _const_spec(shape):
    nd = len(shape)
    return pl.BlockSpec(shape, lambda *a: (0,) * nd, pipeline_mode=pl.Buffered(1))


def _ada_kernel(c_ref, w_ref, b_ref, o_ref):
    c = c_ref[...]
    a = (c * jax.nn.sigmoid(c)).astype(BF)
    o_ref[0] = _dot(a, w_ref[0].astype(BF)) + b_ref[0]


def _ada_call(c_all, ada_w, ada_b):
    depth = ada_w.shape[0]
    nb = c_all.shape[0]
    return pl.pallas_call(
        _ada_kernel,
        out_shape=jax.ShapeDtypeStruct((depth, nb, 6 * D_MODEL), F32),
        grid=(depth, 6),
        in_specs=[
            pl.BlockSpec((nb, D_MODEL), lambda l, j: (0, 0)),
            pl.BlockSpec((1, D_MODEL, D_MODEL), lambda l, j: (l, 0, j)),
            pl.BlockSpec((1, 1, D_MODEL), lambda l, j: (l, 0, j)),
        ],
        out_specs=pl.BlockSpec((1, nb, D_MODEL), lambda l, j: (l, 0, j)),
        compiler_params=_cparams(("arbitrary", "arbitrary")),
    )(c_all, ada_w, ada_b.reshape(depth, 1, 6 * D_MODEL))


def _proj_kernel(x_ref, sc_ref, sh_ref, g_ref, w_ref, kv_ref, rest_ref, kvb_ref, kvt_ref):
    h = _norm_mod(x_ref[0], g_ref[...], sc_ref[0, 0, 0], sh_ref[0, 0, 0]).astype(BF)
    y = _dot(h, w_ref[...])
    kv_ref[0] = y[:, P_KV:P_KV + KV_W]
    rest_ref[0] = y[:, KV_W:]
    att = jnp.concatenate([y[:, P_KV:P_KV + KV_W], y[:, P_WIN:P_WIN + WIN_W]], axis=1)
    kvb_ref[0] = att.astype(BF)
    kvt_ref[0] = att.T.astype(BF)


def _mod_spec(l, which, rm):
    return pl.BlockSpec((1, 1, 1, rm, D_MODEL), lambda b, i: (l, b, which, 0, 0))


def _proj_call(l, x3, mod, g, w_packed, tm):
    bx, tx, _ = x3.shape
    rm = mod.shape[3]
    return pl.pallas_call(
        _proj_kernel,
        out_shape=(jax.ShapeDtypeStruct((bx, tx, KV_W), F32), jax.ShapeDtypeStruct((bx, tx, R_W), F32),
                   jax.ShapeDtypeStruct((bx, tx, KVB_W), BF), jax.ShapeDtypeStruct((bx, KVB_W, tx), BF)),
        grid=(bx, tx // tm),
        in_specs=[
            pl.BlockSpec((1, tm, D_MODEL), lambda b, i: (b, i, 0)),
            _mod_spec(l, 1, rm), _mod_spec(l, 0, rm),
            _const_spec((1, D_MODEL)),
            _const_spec((D_MODEL, P_W)),
        ],
        out_specs=(pl.BlockSpec((1, tm, KV_W), lambda b, i: (b, i, 0)),
                   pl.BlockSpec((1, tm, R_W), lambda b, i: (b, i, 0)),
                   pl.BlockSpec((1, tm, KVB_W), lambda b, i: (b, i, 0)),
                   pl.BlockSpec((1, KVB_W, tm), lambda b, i: (b, 0, i))),
        compiler_params=_cparams(("arbitrary", "arbitrary")),
    )(x3, mod, mod, g, w_packed)


def _compress_finish(y, yb, w2):
    n = y.shape[0]
    yn = pltpu.roll(y, n - 1, axis=0)
    hid = CMP_HID
    hk = y[:, 0:hid] + yn[:, hid:2 * hid] + (yb[0:1, 0:hid] + yb[1:2, hid:2 * hid])
    hv = y[:, 2 * hid:3 * hid] + yn[:, 3 * hid:4 * hid] + (yb[0:1, 2 * hid:3 * hid] + yb[1:2, 3 * hid:4 * hid])
    act = jnp.concatenate([jax.nn.gelu(hk), jax.nn.gelu(hv)], axis=1).astype(BF)
    return _dot(act, w2)


def _compress_kernel(ch_ref, wc_ref, pe_ref, w2_ref, o_ref):
    y = _dot(ch_ref[0], wc_ref[...])
    yb = _dot(pe_ref[...], wc_ref[...])
    o_ref[0] = _compress_finish(y, yb, w2_ref[...]).astype(BF)


def _compress_call(chunks, wc, pe2, w2):
    b, n_ch, cw = chunks.shape
    return pl.pallas_call(
        _compress_kernel,
        out_shape=jax.ShapeDtypeStruct((b, n_ch, 2 * NSA_HD), BF),
        grid=(b,),
        in_specs=[pl.BlockSpec((1, n_ch, cw), lambda i: (i, 0, 0)),
                  _const_spec(wc.shape), _const_spec(pe2.shape), _const_spec(w2.shape)],
        out_specs=pl.BlockSpec((1, n_ch, 2 * NSA_HD), lambda i: (i, 0, 0)),
        compiler_params=_cparams(("arbitrary",)),
    )(chunks, wc, pe2, w2)


def _lam_value(lam_ref, lam_init):
    lp = lam_ref[0]
    a = jnp.sum(lp[0:1] * lp[1:2], axis=-1, keepdims=True)
    b = jnp.sum(lp[2:3] * lp[3:4], axis=-1, keepdims=True)
    return jnp.exp(a) - jnp.exp(b) + lam_init


def _group_rms(od, gmat, gain, lam_init):
    ms = _dot(od * od, gmat, precision=HI)
    return od * lax.rsqrt(ms + EPS) * gain * (1.0 - lam_init)


def _prompt_attn_kernel(qa_ref, qd_ref, ga_ref, kvb_ref, kvt_ref, kc_ref, kct_ref, ov_ref, aux_ref, qaux_ref,
                        gmat_ref, lam_ref, gain_ref, oa_ref, od_ref, m_sc, l_sc, acc_sc, s_sc, p_sc, a_sc, *,
                        lam_init, top):
    tq = QBLOCK
    kc_len = KEY_CHUNK
    i = pl.program_id(1)
    t0 = i * tq
    n_sel = ov_ref.shape[0]

    t_row = t0 + lax.broadcasted_iota(jnp.int32, (1, LANES), 1)
    feat = lax.broadcasted_iota(jnp.int32, (LANES, LANES), 0)

    qa = qa_ref[0] * (NSA_HD ** -0.5 * LOG2E)
    qa_t = [qa[:, 0:LANES].T, qa[:, LANES:2 * LANES].T]
    slabs = []
    for h in range(NSA_HEADS):
        tile = qa_t[h // 2]
        if h % 2:
            tile = pltpu.roll(tile, NSA_HD, axis=0)
        slabs.append(jnp.where(feat < NSA_HD, tile, 0.0))
    q_nsa = jnp.concatenate(slabs, axis=1).astype(BF)

    qd = qd_ref[0] * (DIFF_QK ** -0.5 * LOG2E)
    q_diff = []
    for g in range(2):
        tile = qd[:, g * LANES:(g + 1) * LANES].T
        slabs = [jnp.where(lax.shift_right_logical(feat, 5) == sl, tile, 0.0) for sl in range(4)]
        q_diff.append(jnp.concatenate(slabs, axis=1).astype(BF))

    def lanes_of(x, sl):
        return x[:, sl * tq:(sl + 1) * tq]

    kc = kc_ref[0]
    n_ch = kc.shape[0]
    s = _dot(kc, q_nsa)
    cmp_end = lax.broadcasted_iota(jnp.int32, (n_ch, LANES), 0) * CMP_STRIDE + (CMP_BLOCK - 1)
    vis = cmp_end <= t_row
    p_heads = []
    for h in range(NSA_HEADS):
        sh = jnp.where(vis, lanes_of(s, h), NEG)
        m = jnp.max(sh, axis=0, keepdims=True)
        e = jnp.where(vis, jnp.exp2(sh - m), 0.0)
        d = jnp.sum(e, axis=0, keepdims=True)
        p_heads.append(e / jnp.where(d > 0, d, 1.0))
    o_cmp = _dot(kct_ref[0], jnp.concatenate(p_heads, axis=1).astype(BF))
    p_sum = p_heads[0] + p_heads[1] + p_heads[2] + p_heads[3]
    imp_t = _dot(ov_ref[...], p_sum, precision=HI)

    j_full = lax.broadcasted_iota(jnp.int32, (n_sel, LANES), 0)
    valid = j_full * SEL_BLOCK <= t_row
    cur = lax.shift_right_logical(t_row, 6)
    forced = jnp.where(j_full == 0, 1, jnp.where(j_full == cur, 1, jnp.where(j_full == cur - 1, 1, 0)))
    score = jnp.where(valid, jnp.where(forced > 0, BIG, imp_t), -BIG)
    rank = jnp.zeros((n_sel, LANES), F32)
    for ii in range(n_sel):
        si = score[ii:ii + 1, :]
        rank = rank + jnp.where(si > score, 1.0, jnp.where(si == score, jnp.where(j_full > ii, 1.0, 0.0), 0.0))
    sel_neg = jnp.where(rank < top, 0.0, NEG)

    m_sc[...] = jnp.full(m_sc.shape, NEG, F32)
    l_sc[...] = jnp.zeros(l_sc.shape, F32)
    acc_sc[...] = jnp.zeros(acc_sc.shape, F32)

    aux_nsa = qaux_ref[0]
    pad = [jnp.zeros((AUX_SEL_ROW - n_sel, 4 * tq), F32)] if n_sel < AUX_SEL_ROW else []
    aux_sel = jnp.concatenate([aux_nsa[0:AUX_SEL_ROW], jnp.concatenate([sel_neg] * 4, axis=1)] + pad, axis=0)
    q_win = jnp.concatenate([q_nsa, aux_nsa.astype(BF)], axis=0)
    k_tile = [1, 2, 3]
    v_tile = [1, 4, 5]
    q_grp = [jnp.concatenate([q_nsa, aux_sel.astype(BF)], axis=0),
             jnp.concatenate([q_diff[0], qaux_ref[1].astype(BF)], axis=0),
             jnp.concatenate([q_diff[1], qaux_ref[2].astype(BF)], axis=0)]
    n_full = t0 // kc_len

    def scores(c, buf):
        ks = pl.multiple_of(c * kc_len, kc_len)
        aux_k = aux_ref[pl.ds(ks, kc_len), :]
        for grp in range(3):
            k_op = jnp.concatenate(
                [kvb_ref[0, pl.ds(ks, kc_len), k_tile[grp] * LANES:(k_tile[grp] + 1) * LANES], aux_k], axis=1)
            s_sc[buf, grp] = _dot(k_op, q_grp[grp])

    def softmax(c, buf, diagonal):
        if diagonal:
            pos = c * kc_len + lax.broadcasted_iota(jnp.int32, (kc_len, LANES), 0)
            causal_bias = jnp.where(pos <= t_row, 0.0, NEG)
        for grp in range(3):
            m_old_all = m_sc[grp]
            l_old_all = l_sc[grp]
            alphas, ms, ls = [], [], []
            for sl in range(4):
                s_h = s_sc[buf, grp, :, sl * tq:(sl + 1) * tq]
                if diagonal:
                    s_h = s_h + causal_bias
                m_old = lanes_of(m_old_all, sl)
                m_new = jnp.maximum(m_old, jnp.max(s_h, axis=0, keepdims=True))
                alpha = jnp.exp2(m_old - m_new)
                e = jnp.exp2(s_h - m_new)
                ls.append(alpha * lanes_of(l_old_all, sl) + jnp.sum(e, axis=0, keepdims=True))
                ms.append(m_new)
                alphas.append(alpha)
                p_sc[buf, grp, :, sl * tq:(sl + 1) * tq] = e.astype(BF)
            m_sc[grp] = jnp.concatenate(ms, axis=1)
            l_sc[grp] = jnp.concatenate(ls, axis=1)
            a_sc[buf, grp] = jnp.concatenate(alphas, axis=1)

    def values(c, buf):
        ks = pl.multiple_of(jnp.maximum(c, 0) * kc_len, kc_len)
        for grp in range(3):
            v_t = kvt_ref[0, v_tile[grp] * LANES:(v_tile[grp] + 1) * LANES, pl.ds(ks, kc_len)]
            acc_sc[grp] = a_sc[buf, grp] * acc_sc[grp] + _dot(v_t, p_sc[buf, grp])

    def stage(c, buf):
        scores(c + 1, 1 - buf)
        values(c - 1, 1 - buf)
        softmax(c, buf, False)

    def last(c, buf):
        values(c - 1, 1 - buf)
        softmax(c, buf, True)
        values(c, buf)

    scores(0, 0)
    p_sc[1] = jnp.zeros(p_sc.shape[1:], BF)
    a_sc[1] = jnp.ones(a_sc.shape[1:], F32)
    n_pairs = n_full // 2

    def pair(k, carry):
        stage(2 * k, 0)
        stage(2 * k + 1, 1)
        return carry

    lax.fori_loop(0, n_pairs, pair, 0)

    @pl.when(n_full == 2 * n_pairs)
    def _():
        last(n_full, 0)

    @pl.when(n_full != 2 * n_pairs)
    def _():
        stage(n_full - 1, 0)
        last(n_full, 1)

    o_sel = acc_sc[0] / l_sc[0]

    span = WINDOW + tq
    start = pl.multiple_of(jnp.maximum(i - WINDOW // tq, 0) * tq, tq)
    pos_w = start + lax.broadcasted_iota(jnp.int32, (span, LANES), 0)
    dist = t_row - pos_w
    mb_w = jnp.where(dist >= 0, jnp.where(dist < WINDOW, 0.0, NEG), NEG)
    k_win = jnp.concatenate([kvb_ref[0, pl.ds(start, span), KV_W:KV_W + WIN_W], aux_ref[pl.ds(start, span), :]],
                            axis=1)
    sw = _dot(k_win, q_win)
    pw = []
    for h in range(NSA_HEADS):
        s_h = lanes_of(sw, h) + mb_w
        e = jnp.exp2(s_h - jnp.max(s_h, axis=0, keepdims=True))
        pw.append((e / jnp.sum(e, axis=0, keepdims=True)).astype(BF))
    o_win = _dot(kvt_ref[0, KV_W:KV_W + WIN_W, pl.ds(start, span)], jnp.concatenate(pw, axis=1))

    ga_t = jax.nn.sigmoid(ga_ref[0]).T
    oa_h = []
    for h in range(NSA_HEADS):
        oa_h.append(ga_t[3 * h:3 * h + 1] * lanes_of(o_cmp, h) + ga_t[3 * h + 1:3 * h + 2] * lanes_of(o_sel, h)
                    + ga_t[3 * h + 2:3 * h + 3] * lanes_of(o_win, h))
    oa_ref[0] = jnp.concatenate(
        [jnp.concatenate([oa_h[0][NSA_HD:], oa_h[1][NSA_HD:]], axis=0).T,
         jnp.concatenate([oa_h[2][NSA_HD:], oa_h[3][NSA_HD:]], axis=0).T], axis=1)

    lam = _lam_value(lam_ref, lam_init)
    tiles = []
    for g in range(2):
        on = acc_sc[1 + g] / l_sc[1 + g]
        a0 = lanes_of(on, 0) - lam * lanes_of(on, 1)
        a1 = lanes_of(on, 2) - lam * lanes_of(on, 3)
        tiles.append(jnp.concatenate([a0[:DIFF_V], a1[DIFF_V:]], axis=0).T)
    od = jnp.concatenate(tiles, axis=1)
    od_ref[0] = _group_rms(od, gmat_ref[...], gain_ref[0], lam_init)


def _prompt_attn_call(l, proj, kvb, kvt, kcvc, kct, ov, aux, qaux, gmat, diff_lam, gain, lam_init):
    b, t, _ = proj.shape
    n_ch = kcvc.shape[1]
    n_sel = ov.shape[0]
    tq = QBLOCK
    assert n_sel <= LANES - AUX_SEL_ROW and KEY_CHUNK % tq == 0
    kern = functools.partial(_prompt_attn_kernel, lam_init=lam_init, top=min(SEL_TOP_N, n_sel))
    return pl.pallas_call(
        kern,
        out_shape=(jax.ShapeDtypeStruct((b, t, BRANCH_W), F32), jax.ShapeDtypeStruct((b, t, BRANCH_W), F32)),
        grid=(b, t // tq),
        in_specs=[
            pl.BlockSpec((1, tq, 256), lambda bi, i: (bi, i, R_QA // 256)),
            pl.BlockSpec((1, tq, 256), lambda bi, i: (bi, i, R_QD // 256)),
            pl.BlockSpec((1, tq, LANES), lambda bi, i: (bi, i, R_GA // LANES)),
            pl.BlockSpec((1, t, KVB_W), lambda bi, i: (bi, 0, 0)),
            pl.BlockSpec((1, KVB_W, t), lambda bi, i: (bi, 0, 0)),
            pl.BlockSpec((1, n_ch, 2 * NSA_HD), lambda bi, i: (bi, 0, 0)),
            pl.BlockSpec((1, 2 * NSA_HD, n_ch), lambda bi, i: (bi, 0, 0)),
            _const_spec(ov.shape), _const_spec(aux.shape), _const_spec(qaux.shape), _const_spec(gmat.shape),
            pl.BlockSpec((1, 4, DIFF_QK), lambda bi, i: (l, 0, 0)),
            pl.BlockSpec((1, 1, BRANCH_W), lambda bi, i: (l, 0, 0)),
        ],
        out_specs=(pl.BlockSpec((1, tq, BRANCH_W), lambda bi, i: (bi, i, 0)),
                   pl.BlockSpec((1, tq, BRANCH_W), lambda bi, i: (bi, i, 0))),
        scratch_shapes=[pltpu.VMEM((3, 1, 4 * tq), F32), pltpu.VMEM((3, 1, 4 * tq), F32),
                        pltpu.VMEM((3, LANES, 4 * tq), F32),
                        pltpu.VMEM((2, 3, KEY_CHUNK, 4 * tq), F32), pltpu.VMEM((2, 3, KEY_CHUNK, 4 * tq), BF),
                        pltpu.VMEM((2, 3, 1, 4 * tq), F32)],
        compiler_params=_cparams(("arbitrary", "arbitrary")),
    )(proj, proj, proj, kvb, kvt, kcvc, kct, ov, aux, qaux, gmat, diff_lam, gain)


def _sample_attn_seq(kvn_ref, rest_ref, *refs, n_pages, lam_init, top, n_sel):
    pages = refs[:n_pages]
    (swin_ref, wc_ref, pe_ref, w2_ref, ov_ref, e_ref, gmat_ref, lam_ref, gain_ref, oa_ref, od_ref,
     cmp_rows) = refs[n_pages:]
    past = n_pages * PAGE_SIZE
    n_chp = past // CMP_STRIDE
    n_vis = (past - (CMP_BLOCK - 1)) // CMP_STRIDE + 1
    cur = past // SEL_BLOCK
    nsa_slopes = _slopes(NSA_HEADS)
    diff_slopes = _slopes(DIFF_HEADS)

    row = jnp.concatenate([kvn_ref[0], rest_ref[0]], axis=1)
    kv_new = row[:, P_KV:P_KV + KV_W]
    lane = lax.broadcasted_iota(jnp.int32, (1, LANES), 1)
    lo = lane < NSA_HD
    rid = lax.broadcasted_iota(jnp.int32, (SUBLANES, 1), 0)
    rid_full = lax.broadcasted_iota(jnp.int32, (SUBLANES, LANES), 0)

    def bcast(v):
        return jnp.broadcast_to(v, (SUBLANES, v.shape[1]))

    qa = row[:, P_QA:P_QA + 256] * (NSA_HD ** -0.5)
    b0, b1 = bcast(qa[:, 0:LANES]), bcast(qa[:, LANES:2 * LANES])
    cand = [b0, pltpu.roll(b0, NSA_HD, axis=1), b1, pltpu.roll(b1, NSA_HD, axis=1)]
    q8 = jnp.zeros((SUBLANES, LANES), F32)
    for h in range(NSA_HEADS):
        q8 = jnp.where(rid_full == h, cand[h], q8)
    q8 = jnp.where(lo, q8, 0.0)
    q8b = q8.astype(BF)
    slope_col = jnp.zeros((SUBLANES, 1), F32)
    for h in range(NSA_HEADS):
        slope_col = jnp.where(rid == h, nsa_slopes[h], slope_col)

    for j, pg in enumerate(pages):
        cmp_rows[j * PAGE_SIZE:(j + 1) * PAGE_SIZE, :] = pg[0, 0, :, 0:LANES]
    chunks = jnp.concatenate([cmp_rows[pl.ds(r, n_chp, stride=CMP_STRIDE), :].astype(BF)
                              for r in range(CMP_STRIDE)], axis=1)
    y = yield chunks
    yb = _dot(pe_ref[...], wc_ref[...])
    kc = _compress_finish(y, yb, w2_ref[...]).astype(BF)
    yield

    cl = lax.broadcasted_iota(jnp.int32, (1, n_chp), 1)
    vis = cl < n_vis
    s = jnp.where(vis, _dot_nt(q8b, kc), NEG)
    m = jnp.max(s, axis=-1, keepdims=True)
    e = jnp.where(vis, jnp.exp(s - m), 0.0)
    d = jnp.sum(e, axis=-1, keepdims=True)
    p_c = jnp.where(rid < NSA_HEADS, e / jnp.where(d > 0, d, 1.0), 0.0)
    o_cmp = _dot(p_c.astype(BF), kc)
    imp = jnp.sum(_dot(p_c, ov_ref[...], precision=HI), axis=0, keepdims=True)
    yield

    forced = jnp.where(lane == 0, 1, jnp.where(lane == cur, 1, jnp.where(lane == cur - 1, 1, 0)))
    score = jnp.where(lane < n_sel, jnp.where(forced > 0, BIG, imp), -BIG)
    r_row = jnp.broadcast_to(score, (LANES, LANES))
    r_col = r_row.T
    ii = lax.broadcasted_iota(jnp.int32, (LANES, LANES), 0)
    jj = lax.broadcasted_iota(jnp.int32, (LANES, LANES), 1)
    inc = jnp.where(r_col > r_row, 1.0, jnp.where(r_col == r_row, jnp.where(ii < jj, 1.0, 0.0), 0.0))
    rank = jnp.sum(inc, axis=0, keepdims=True)
    sel = jnp.where(rank < top, 1.0, 0.0)
    sel_exp = _dot(bcast(sel).astype(BF), e_ref[...])
    yield

    pos = lax.broadcasted_iota(jnp.int32, (1, past), 1)
    rel = (pos - past).astype(F32)

    def attend_new(q, sc, k_tile, kv_tile_new):
        s_new = jnp.sum(q * kv_tile_new, axis=-1, keepdims=True)
        mm = jnp.maximum(jnp.max(sc, axis=-1, keepdims=True), s_new)
        ee = jnp.exp(sc - mm)
        e_new = jnp.exp(s_new - mm)
        den = jnp.sum(ee, axis=-1, keepdims=True) + e_new
        return (_dot(ee.astype(BF), k_tile) + e_new * kv_tile_new) / den

    k_sel = jnp.concatenate([pg[0, 0, :, LANES:2 * LANES] for pg in pages], axis=0).astype(BF)
    sc = _dot_nt(q8b, k_sel) + slope_col * rel + jnp.where(sel_exp > 0.5, 0.0, NEG)
    o_sel = attend_new(q8, sc, k_sel, kv_new[:, LANES:2 * LANES])
    yield

    k_win = swin_ref[0, 0].astype(BF)
    nw = k_win.shape[0]
    wl = lax.broadcasted_iota(jnp.int32, (1, nw), 1)
    dist_w = nw - wl
    sc = _dot_nt(q8b, k_win) - slope_col * dist_w.astype(F32) + jnp.where(dist_w < WINDOW, 0.0, NEG)
    o_win = attend_new(q8, sc, k_win, row[:, P_WIN:P_WIN + WIN_W])

    ga8 = bcast(jax.nn.sigmoid(row[:, P_GA:P_GA + LANES]))
    gates = [jnp.sum(jnp.where(lane == 3 * rid_full + k, ga8, 0.0), axis=-1, keepdims=True) for k in range(3)]
    oa8 = gates[0] * o_cmp + gates[1] * o_sel + gates[2] * o_win
    oa_ref[0] = jnp.concatenate(
        [jnp.where(lo, pltpu.roll(oa8[0:1], NSA_HD, axis=1), oa8[1:2]),
         jnp.where(lo, pltpu.roll(oa8[2:3], NSA_HD, axis=1), oa8[3:4])], axis=1)
    yield

    qd = bcast(row[:, P_QD:P_QD + 256] * (DIFF_QK ** -0.5))
    lane2 = lax.broadcasted_iota(jnp.int32, (SUBLANES, 2 * LANES), 1)
    rid2 = lax.broadcasted_iota(jnp.int32, (SUBLANES, 2 * LANES), 0)
    qd8 = jnp.where(lax.shift_right_logical(lane2, 5) == rid2, qd, 0.0)
    dslope = jnp.zeros((SUBLANES, 1), F32)
    for r in range(SUBLANES):
        dslope = jnp.where(rid == r, diff_slopes[r // 2], dslope)
    k_d = jnp.concatenate([pg[0, 0, :, 2 * LANES:4 * LANES] for pg in pages], axis=0).astype(BF)
    v_d = jnp.concatenate([pg[0, 0, :, 4 * LANES:6 * LANES] for pg in pages], axis=0).astype(BF)
    sc = _dot_nt(qd8.astype(BF), k_d) + dslope * rel
    s_new = jnp.sum(qd8 * kv_new[:, 2 * LANES:4 * LANES], axis=-1, keepdims=True)
    mm = jnp.maximum(jnp.max(sc, axis=-1, keepdims=True), s_new)
    ee = jnp.exp(sc - mm)
    e_new = jnp.exp(s_new - mm)
    den = jnp.sum(ee, axis=-1, keepdims=True) + e_new
    o8 = (_dot(ee.astype(BF), v_d) + e_new * kv_new[:, 4 * LANES:6 * LANES]) / den
    lam = _lam_value(lam_ref, lam_init)
    coef = jnp.where((rid2 & 1) == 0, 1.0, -lam)
    head_mask = lax.shift_right_logical(lane2, 6) == lax.shift_right_logical(rid2, 1)
    od = jnp.sum(jnp.where(head_mask, coef * o8, 0.0), axis=0, keepdims=True)
    od8 = _group_rms(bcast(od), gmat_ref[...], gain_ref[0], lam_init)
    od_ref[0] = od8[0:1]


def _sample_group_kernel(pt_ref, kvn_ref, rest_ref, *refs, n_pages, group, **kw):
    pages = refs[:group * n_pages]
    swin_ref, *consts, oa_ref, od_ref, cmp_rows = refs[group * n_pages:]
    seqs = []
    for s in range(group):
        one = pl.ds(s, 1)
        seqs.append(_sample_attn_seq(kvn_ref.at[one], rest_ref.at[one],
                                     *pages[s * n_pages:(s + 1) * n_pages], swin_ref.at[:, one],
                                     *consts, oa_ref.at[one], od_ref.at[one], cmp_rows.at[s], n_pages=n_pages, **kw))
    chunks = [next(seq) for seq in seqs]
    n_chp = chunks[0].shape[0]
    y_all = _dot(jnp.concatenate(chunks, axis=0), consts[0][...])
    for s, seq in enumerate(seqs):
        seq.send(y_all[s * n_chp:(s + 1) * n_chp])
    live = True
    while live:
        live = False
        for seq in seqs:
            live = next(seq, "done") != "done" or live


def _sample_attn_call(l, page_table, kv_new, rest, cache_kv, state_win, wc, pe2, w2, ov, emat, gmat, diff_lam, gain,
                      lam_init, n_sel):
    bs = rest.shape[0]
    n_pages = page_table.shape[1]
    nw = state_win.shape[2]
    group = SAMPLE_GROUP if bs % SAMPLE_GROUP == 0 else 1
    kern = functools.partial(_sample_group_kernel, n_pages=n_pages, group=group, lam_init=lam_init,
                             top=min(SEL_TOP_N, n_sel), n_sel=n_sel)

    def page_spec(s, j):
        return pl.BlockSpec((1, 1, PAGE_SIZE, KV_W), lambda b, pt: (l, pt[group * b + s, j], 0, 0))

    def cspec(shape):
        nd = len(shape)
        return pl.BlockSpec(shape, lambda b, pt: (0,) * nd, pipeline_mode=pl.Buffered(1))

    grid_spec = pltpu.PrefetchScalarGridSpec(
        num_scalar_prefetch=1,
        grid=(bs // group,),
        in_specs=[pl.BlockSpec((group, 1, KV_W), lambda b, pt: (b, 0, 0)),
                  pl.BlockSpec((group, 1, R_W), lambda b, pt: (b, 0, 0))]
        + [page_spec(s, j) for s in range(group) for j in range(n_pages)]
        + [pl.BlockSpec((1, group, nw, WIN_W), lambda b, pt: (l, b, 0, 0)),
           cspec(wc.shape), cspec(pe2.shape), cspec(w2.shape), cspec(ov.shape), cspec(emat.shape),
           cspec(gmat.shape),
           pl.BlockSpec((1, 4, DIFF_QK), lambda b, pt: (l, 0, 0)),
           pl.BlockSpec((1, 1, BRANCH_W), lambda b, pt: (l, 0, 0))],
        out_specs=(pl.BlockSpec((group, 1, BRANCH_W), lambda b, pt: (b, 0, 0)),
                   pl.BlockSpec((group, 1, BRANCH_W), lambda b, pt: (b, 0, 0))),
        scratch_shapes=[pltpu.VMEM((group, n_pages * PAGE_SIZE, LANES), F32)],
    )
    return pl.pallas_call(
        kern,
        out_shape=(jax.ShapeDtypeStruct((bs, 1, BRANCH_W), F32), jax.ShapeDtypeStruct((bs, 1, BRANCH_W), F32)),
        grid_spec=grid_spec,
        compiler_params=_cparams(("arbitrary",)),
    )(page_table, kv_new, rest, *([cache_kv] * (group * n_pages)), state_win, wc, pe2, w2, ov, emat, gmat, diff_lam, gain)


def _merge_kernel(*refs, halo):
    if halo:
        (x_ref, sc_ref, sh_ref, gt_ref, g_ref, cv_ref, ub_ref, cvp_ref, ubp_ref, oa_ref, od_ref,
         wg_ref, wb_ref, wo_ref, pw_ref, ps_ref, cw_ref, xo_ref, zin_ref) = refs
    else:
        (x_ref, sc_ref, sh_ref, gt_ref, g_ref, cv_ref, ub_ref, stc_ref, stp_ref, oa_ref, od_ref,
         wg_ref, wb_ref, wo_ref, pw_ref, ps_ref, cw_ref, xo_ref, zin_ref) = refs
    i = pl.program_id(1)
    x = x_ref[0]
    tm = x.shape[0]
    h = _norm_mod(x, g_ref[...], sc_ref[0, 0, 0], sh_ref[0, 0, 0]).astype(BF)

    cv = cv_ref[0]
    u = ub_ref[0]
    bg = cv[:, 0:CONV_W]
    zin = cv[:, CONV_W:2 * CONV_W] * cv[:, 2 * CONV_W:3 * CONV_W]
    zin_ref[0] = zin
    cw = cw_ref[...]
    lane = lax.broadcasted_iota(jnp.int32, (1, POOL_W), 1)
    win_lane = jnp.where(lane < 64, 2, jnp.where(lane < 128, 4, jnp.where(lane < 192, 8, 16)))

    if halo:
        keep = jnp.where(i > 0, 1.0, 0.0)
        cvp = cvp_ref[0] * keep
        ubp = ubp_ref[0] * keep
        z_ext = jnp.concatenate([cvp[:, CONV_W:2 * CONV_W] * cvp[:, 2 * CONV_W:3 * CONV_W], zin], axis=0)
        z = (cw[0:1] * pltpu.roll(z_ext, 2, axis=0) + cw[1:2] * pltpu.roll(z_ext, 1, axis=0)
             + cw[2:3] * z_ext)[HALO:]
        ext = jnp.concatenate([ubp, u], axis=0)
        s2 = ext + pltpu.roll(ext, 1, axis=0)
        s4 = s2 + pltpu.roll(s2, 2, axis=0)
        s8 = s4 + pltpu.roll(s4, 4, axis=0)
        s16 = s8 + pltpu.roll(s8, 8, axis=0)
        tot = jnp.where(lane < 64, s2, jnp.where(lane < 128, s4, jnp.where(lane < 192, s8, s16)))[HALO:]
        pos1 = i * tm + lax.broadcasted_iota(jnp.int32, (tm, 1), 0) + 1
        cnt = jnp.minimum(win_lane, pos1).astype(F32)
    else:
        z = cw[0:1] * stc_ref[0] + cw[1:2] * stc_ref[1] + cw[2:3] * zin
        s2 = u + stp_ref[POOL_KEEP - 1]
        s4 = s2 + stp_ref[POOL_KEEP - 2] + stp_ref[POOL_KEEP - 3]
        s8 = s4
        for k in range(4, 8):
            s8 = s8 + stp_ref[POOL_KEEP - k]
        s16 = s8
        for k in range(8, 16):
            s16 = s16 + stp_ref[POOL_KEEP - k]
        tot = jnp.where(lane < 64, s2, jnp.where(lane < 128, s4, jnp.where(lane < 192, s8, s16)))
        cnt = win_lane.astype(F32)
    o_c = bg * z
    pooled = tot / cnt - u
    o_b = _dot(pooled.astype(BF), pw_ref[...]) * ps_ref[...]

    branches = [oa_ref[0], o_b, o_c, od_ref[0]]
    mixed = jnp.zeros((tm, D_MODEL), F32)
    for n in range(N_BRANCH):
        pb = _dot(branches[n].astype(BF), wb_ref[n])
        gate = jax.nn.sigmoid(_dot(h, wg_ref[:, n * D_MODEL:(n + 1) * D_MODEL]))
        mixed = mixed + gate * pb
    xo_ref[0] = x + gt_ref[0, 0, 0] * _dot(mixed.astype(BF), wo_ref[...])


def _merge_call(l, x3, mod, g, proj, prev_c, prev_p, o_a, o_d, wg, wb, wo, pw, ps, cw, tm, halo):
    bx, tx, _ = x3.shape
    rm = mod.shape[3]
    if halo:
        nblk = tm // HALO
        prev_specs = [
            pl.BlockSpec((1, HALO, 3 * CONV_W), lambda b, i: (b, jnp.maximum(i * nblk - 1, 0), R_CV // (3 * CONV_W))),
            pl.BlockSpec((1, HALO, POOL_W), lambda b, i: (b, jnp.maximum(i * nblk - 1, 0), R_UB // POOL_W)),
        ]
    else:
        prev_specs = [pl.BlockSpec(prev_c.shape, lambda b, i: (0, 0, 0)),
                      pl.BlockSpec(prev_p.shape, lambda b, i: (0, 0, 0))]
    return pl.pallas_call(
        functools.partial(_merge_kernel, halo=halo),
        out_shape=(jax.ShapeDtypeStruct((bx, tx, D_MODEL), F32), jax.ShapeDtypeStruct((bx, tx, CONV_W), F32)),
        grid=(bx, tx // tm),
        in_specs=[
            pl.BlockSpec((1, tm, D_MODEL), lambda b, i: (b, i, 0)),
            _mod_spec(l, 1, rm), _mod_spec(l, 0, rm), _mod_spec(l, 2, rm),
            _const_spec((1, D_MODEL)),
            pl.BlockSpec((1, tm, 3 * CONV_W), lambda b, i: (b, i, R_CV // (3 * CONV_W))),
            pl.BlockSpec((1, tm, POOL_W), lambda b, i: (b, i, R_UB // POOL_W)),
            *prev_specs,
            pl.BlockSpec((1, tm, BRANCH_W), lambda b, i: (b, i, 0)),
            pl.BlockSpec((1, tm, BRANCH_W), lambda b, i: (b, i, 0)),
            _const_spec(wg.shape), _const_spec(wb.shape), _const_spec(wo.shape),
            _const_spec(pw.shape), _const_spec(ps.shape), _const_spec(cw.shape),
        ],
        out_specs=(pl.BlockSpec((1, tm, D_MODEL), lambda b, i: (b, i, 0)),
                   pl.BlockSpec((1, tm, CONV_W), lambda b, i: (b, i, 0))),
        compiler_params=_cparams(("arbitrary", "arbitrary")),
    )(x3, mod, mod, mod, g, proj, proj, prev_c, prev_p, o_a, o_d, wg, wb, wo, pw, ps, cw)


def _ffn_kernel(*refs, halo, final, keep_rows):
    (x_ref, sc_ref, sh_ref, gt_ref, g_ref, prev_ref, wu_ref, fc_ref, wd_ref, fg_ref, xo_ref, up_ref) = refs
    i = pl.program_id(1)
    x = x_ref[0]
    tm = x.shape[0]
    g = g_ref[...]
    sc = sc_ref[0, 0, 0]
    sh = sh_ref[0, 0, 0]
    if halo:
        hx = jnp.concatenate([prev_ref[0], x], axis=0)
        h2 = _norm_mod(hx, g, sc, sh).astype(BF)
        rows = lax.broadcasted_iota(jnp.int32, (tm + FFN_HALO, 1), 0)
        live = jnp.where(rows >= FFN_HALO, 1.0, jnp.where(i > 0, 1.0, 0.0))
    else:
        h2 = _norm_mod(x, g, sc, sh).astype(BF)
    acc = jnp.zeros((tm, D_MODEL), F32)
    for c in range(D_FF // FF_CHUNK):
        halves = []
        for part in range(2):
            col = part * D_FF + c * FF_CHUNK
            up = _dot(h2, wu_ref[:, col:col + FF_CHUNK])
            w = fc_ref[:, col:col + FF_CHUNK]
            if halo:
                up = up * live
                up_ref[0, :, col:col + FF_CHUNK] = up[FFN_HALO + tm - keep_rows:]
                upc = (w[0:1] * pltpu.roll(up, 2, axis=0) + w[1:2] * pltpu.roll(up, 1, axis=0)
                       + w[2:3] * up)[FFN_HALO:]
            else:
                up_ref[0, :, col:col + FF_CHUNK] = up
                upc = (w[0:1] * prev_ref[0, :, col:col + FF_CHUNK] + w[1:2] * prev_ref[1, :, col:col + FF_CHUNK]
                       + w[2:3] * up)
            halves.append(upc)
        val, gg = halves
        act = (gg * jax.nn.sigmoid(gg) * val).astype(BF)
        acc = acc + _dot(act, wd_ref[c * FF_CHUNK:(c + 1) * FF_CHUNK, :])
    out = x + gt_ref[0, 0, 0] * acc
    if final:
        ms = jnp.mean(out * out, axis=-1, keepdims=True)
        out = out * lax.rsqrt(ms + EPS) * fg_ref[...]
    xo_ref[0] = out


def _ffn_call(l, x3, mod, g, prev, wu, fc, wd, fg, tm, halo, final):
    bx, tx, _ = x3.shape
    rm = mod.shape[3]
    keep_rows = SUBLANES if halo else tm
    if halo:
        nblk = tm // FFN_HALO
        prev_spec = pl.BlockSpec((1, FFN_HALO, D_MODEL), lambda b, i: (b, jnp.maximum(i * nblk - 1, 0), 0))
    else:
        prev_spec = pl.BlockSpec(prev.shape, lambda b, i: (0, 0, 0))
    return pl.pallas_call(
        functools.partial(_ffn_kernel, halo=halo, final=final, keep_rows=keep_rows),
        out_shape=(jax.ShapeDtypeStruct((bx, tx, D_MODEL), F32),
                   jax.ShapeDtypeStruct((bx, keep_rows, 2 * D_FF), F32)),
        grid=(bx, tx // tm),
        in_specs=[
            pl.BlockSpec((1, tm, D_MODEL), lambda b, i: (b, i, 0)),
            _mod_spec(l, 4, rm), _mod_spec(l, 3, rm), _mod_spec(l, 5, rm),
            _const_spec((1, D_MODEL)),
            prev_spec,
            _const_spec(wu.shape), _const_spec(fc.shape), _const_spec(wd.shape), _const_spec((1, D_MODEL)),
        ],
        out_specs=(pl.BlockSpec((1, tm, D_MODEL), lambda b, i: (b, i, 0)),
                   pl.BlockSpec((1, keep_rows, 2 * D_FF), lambda b, i: (b, 0, 0))),
        compiler_params=_cparams(("arbitrary", "arbitrary")),
    )(x3, mod, mod, mod, g, prev, wu, fc, wd, fg)


def _pack_w_in(w_in):
    o = np.cumsum([0, 256, KV_W, WIN_W, GA_W, POOL_W, 3 * CONV_W, 256, N_BRANCH * D_MODEL])
    q_a, kv, win, g_a, u_b, cv, q_d, gates = [w_in[..., o[k]:o[k + 1]] for k in range(8)]
    g_a = jnp.pad(g_a, ((0, 0), (0, 0), (0, LANES - GA_W)))
    packed = jnp.concatenate([kv, cv, q_a, u_b, q_d, win, g_a], axis=-1).astype(BF)
    return packed, gates.astype(BF)


def _compress_weights(cmp_pe, cmp_w1, cmp_w2):
    depth = cmp_w1.shape[0]
    half = CMP_STRIDE * NSA_HD
    w1 = cmp_w1.reshape(depth, 2, 2, CMP_STRIDE, NSA_HD, CMP_HID)
    z = jnp.zeros((depth, CMP_STRIDE, NSA_HD, CMP_HID), cmp_w1.dtype)
    cols = []
    for kv in range(2):
        for hf in range(2):
            blk = w1[:, kv, hf]
            pair = (blk, z) if kv == 0 else (z, blk)
            cols.append(jnp.concatenate(pair, axis=2).reshape(depth, 2 * half, CMP_HID))
    wc = jnp.concatenate(cols, axis=-1).astype(BF)
    pe = cmp_pe.reshape(depth, 2, 2, CMP_STRIDE, NSA_HD)
    pe2 = jnp.concatenate([pe[:, 0], pe[:, 1]], axis=-1).reshape(depth, 2, 2 * half)
    pe2 = jnp.pad(pe2, ((0, 0), (0, SUBLANES - 2), (0, 0))).astype(BF)
    zz = jnp.zeros((depth, CMP_HID, NSA_HD), cmp_w2.dtype)
    w2 = jnp.concatenate([jnp.concatenate([cmp_w2[:, 0], zz], axis=-1),
                          jnp.concatenate([zz, cmp_w2[:, 1]], axis=-1)], axis=1).astype(BF)
    return wc, pe2, w2


def _overlap(n_cmp, n_sel):
    cs = np.arange(n_cmp)[:, None] * CMP_STRIDE
    ss = np.arange(n_sel)[None, :] * SEL_BLOCK
    ov = np.minimum(cs + CMP_BLOCK, ss + SEL_BLOCK) - np.maximum(cs, ss)
    return (np.maximum(ov, 0) / CMP_BLOCK).astype(np.float32)


def _expand_matrix(n_rows, n_pos):
    return (np.arange(n_pos)[None, :] // SEL_BLOCK == np.arange(n_rows)[:, None]).astype(np.float32)


def _bf16_terms(x, n):
    terms = []
    for _ in range(n):
        u = np.array(x, np.float32).view(np.uint32)
        t = ((u + ((u >> 16) & 1) + 0x7FFF) & np.uint32(0xFFFF0000)).view(np.float32)
        terms.append(float(t))
        x = x - float(t)
    return terms


def _aux_key_table(n_pos):
    pos = np.arange(n_pos)
    tab = np.zeros((n_pos, LANES), np.float32)
    tab[:, 0:AUX_TERMS] = (pos // SEL_BLOCK)[:, None]
    tab[:, AUX_TERMS:2 * AUX_TERMS] = (pos % SEL_BLOCK)[:, None]
    tab[pos, AUX_SEL_ROW + pos // SEL_BLOCK] = 1.0
    return tab


def _aux_query_rows():
    terms = _bf16_terms(LOG2E, AUX_TERMS)
    slopes = _slopes(NSA_HEADS)
    slab_slope = [[slopes[h] for h in range(4)], [slopes[sl // 2] for sl in range(4)],
                  [slopes[2 + sl // 2] for sl in range(4)]]
    out = np.zeros((3, LANES, 4 * QBLOCK), np.float32)
    for g in range(3):
        for sl in range(4):
            for k, c in enumerate(terms):
                out[g, k, sl * QBLOCK:(sl + 1) * QBLOCK] = SEL_BLOCK * slab_slope[g][sl] * c
                out[g, AUX_TERMS + k, sl * QBLOCK:(sl + 1) * QBLOCK] = slab_slope[g][sl] * c
    return out


def _group_mean_matrix():
    idx = np.arange(DIFF_HEADS * DIFF_V) // DIFF_V
    return (idx[:, None] == idx[None, :]).astype(np.float32) / DIFF_V


def _pool_blockdiag(pool_w):
    depth, n, g, _ = pool_w.shape
    out = jnp.zeros((depth, n * g, n * g), pool_w.dtype)
    for k in range(n):
        out = out.at[:, k * g:(k + 1) * g, k * g:(k + 1) * g].set(pool_w[:, k])
    return out.astype(BF)


def kernel(x_prompt, x_sample, cache_kv, state_win_kv, state_pool, state_conv, state_ffn, page_table, c_prompt, c_sample, norm_g, ada_w, ada_b, w_in, cmp_pe, cmp_w1, cmp_w2, diff_lam, diff_norm_g, pool_w, pool_scale, conv_w, w_branch, w_out, w_up, ffn_conv, w_down, final_g):
    depth = w_in.shape[0]
    bp, seq, _ = x_prompt.shape
    bs = x_sample.shape[0]
    n_pages = page_table.shape[1]
    past = n_pages * PAGE_SIZE
    tm = min(ROW_TILE, seq)
    assert seq % tm == 0 and seq % QBLOCK == 0 and seq >= WINDOW + QBLOCK and x_sample.shape[1] == 1

    w_packed, w_gate = _pack_w_in(w_in)
    wc, pe2, w2 = _compress_weights(cmp_pe, cmp_w1, cmp_w2)
    wb = w_branch.astype(BF)
    wo = w_out.astype(BF)
    wu = w_up.astype(BF)
    wd = w_down.astype(BF)
    pw = _pool_blockdiag(pool_w)
    ps = pool_scale.reshape(depth, 1, POOL_W)
    gain = jnp.tile(diff_norm_g, (1, DIFF_HEADS)).reshape(depth, 1, DIFF_HEADS * DIFF_V)
    fg = final_g.reshape(1, D_MODEL)
    gmat = jnp.asarray(_group_mean_matrix())

    n_ch = seq // CMP_STRIDE
    n_sel_p = seq // SEL_BLOCK
    ov_p = np.zeros((n_sel_p, n_ch), np.float32)
    ov_p[:, :n_ch - 1] = _overlap(n_ch - 1, n_sel_p).T
    ov_p = jnp.asarray(ov_p)
    aux_p = jnp.asarray(_aux_key_table(seq), dtype=BF)
    qaux = jnp.asarray(_aux_query_rows())
    n_sel_s = -(-(past + 1) // SEL_BLOCK)
    n_chp = past // CMP_STRIDE
    ov_s = np.zeros((n_chp, LANES), np.float32)
    ov_s[:, :n_sel_s] = _overlap(n_chp, n_sel_s)
    ov_s = jnp.asarray(ov_s)
    emat_s = jnp.asarray(_expand_matrix(LANES, past), dtype=BF)

    mod = _ada_call(jnp.concatenate([c_prompt, c_sample], axis=0), ada_w, ada_b)
    mod_p = mod[:, :bp].reshape(depth, bp, 6, 1, D_MODEL)
    mod_s = jnp.transpose(mod[:, bp:].reshape(depth, bs, 6, D_MODEL), (0, 2, 1, 3)).reshape(depth, 1, 6, bs, D_MODEL)

    x = x_prompt
    kv_p, win_p, pool_p, conv_p, ffn_p = [], [], [], [], []
    for l in range(depth):
        lam_init = 0.8 - 0.6 * math.exp(-0.3 * l)
        g1 = norm_g[l, 0].reshape(1, D_MODEL)
        g2 = norm_g[l, 1].reshape(1, D_MODEL)
        kv32, rest, kvb, kvt = _proj_call(l, x, mod_p, g1, w_packed[l], tm)
        chunks = kvb[:, :, 0:2 * NSA_HD].reshape(bp, n_ch, CMP_STRIDE * 2 * NSA_HD)
        kcvc = _compress_call(chunks, wc[l], pe2[l], w2[l])
        o_a, o_d = _prompt_attn_call(l, rest, kvb, kvt, kcvc, jnp.swapaxes(kcvc, 1, 2),
                                     ov_p, aux_p, qaux, gmat, diff_lam, gain, lam_init)
        x, zin = _merge_call(l, x, mod_p, g1, rest, rest, rest, o_a, o_d, w_gate[l], wb[l], wo[l], pw[l], ps[l],
                             conv_w[l], tm, True)
        x, up_last = _ffn_call(l, x, mod_p, g2, x, wu[l], ffn_conv[l], wd[l], fg, tm, True, l == depth - 1)
        kv_p.append(kv32)
        win_p.append(rest[:, seq - min(WINDOW, seq):, R_WIN:R_WIN + WIN_W])
        pool_p.append(rest[:, seq - POOL_KEEP:, R_UB:R_UB + POOL_W])
        conv_p.append(zin[:, seq - 2:])
        ffn_p.append(up_last[:, SUBLANES - 2:])
    y_prompt = x

    x = x_sample.reshape(1, bs, D_MODEL)
    kv_s, win_s, pool_s, conv_s, ffn_s = [], [], [], [], []
    for l in range(depth):
        lam_init = 0.8 - 0.6 * math.exp(-0.3 * l)
        g1 = norm_g[l, 0].reshape(1, D_MODEL)
        g2 = norm_g[l, 1].reshape(1, D_MODEL)
        kv32, rest, _, _ = _proj_call(l, x, mod_s, g1, w_packed[l], bs)
        o_a, o_d = _sample_attn_call(l, page_table, kv32.reshape(bs, 1, KV_W), rest.reshape(bs, 1, R_W), cache_kv,
                                     state_win_kv, wc[l], pe2[l], w2[l], ov_s, emat_s, gmat, diff_lam, gain,
                                     lam_init, n_sel_s)
        st_conv = jnp.transpose(state_conv[l], (1, 0, 2))
        st_pool = jnp.transpose(state_pool[l], (1, 0, 2))
        st_ffn = jnp.transpose(state_ffn[l], (1, 0, 2))
        x, zin = _merge_call(l, x, mod_s, g1, rest, st_conv, st_pool, o_a.reshape(1, bs, BRANCH_W),
                             o_d.reshape(1, bs, BRANCH_W), w_gate[l], wb[l], wo[l], pw[l], ps[l], conv_w[l], bs, False)
        x, up_new = _ffn_call(l, x, mod_s, g2, st_ffn, wu[l], ffn_conv[l], wd[l], fg, bs, False, l == depth - 1)
        kv_s.append(kv32.reshape(bs, 1, KV_W))
        win_s.append(rest[0, :, R_WIN:R_WIN + WIN_W].reshape(bs, 1, WIN_W))
        pool_s.append(rest[0, :, R_UB:R_UB + POOL_W].reshape(bs, 1, POOL_W))
        conv_s.append(zin.reshape(bs, 1, CONV_W))
        ffn_s.append(up_new.reshape(bs, 1, 2 * D_FF))
    y_sample = x.reshape(bs, 1, D_MODEL)

    def roll_state(state, new_rows, keep):
        return jnp.concatenate([state[:, :, state.shape[2] - (keep - 1):], jnp.stack(new_rows)], axis=2)

    return (y_prompt, y_sample, jnp.stack(kv_p), jnp.stack(win_p), jnp.stack(pool_p), jnp.stack(conv_p),
            jnp.stack(ffn_p), jnp.stack(kv_s),
            roll_state(state_win_kv, win_s, min(WINDOW, past + 1)), roll_state(state_pool, pool_s, POOL_KEEP),
            roll_state(state_conv, conv_s, 2), roll_state(state_ffn, ffn_s, 2))
```

```python
import functools
import math

import numpy as np
import jax
import jax.numpy as jnp
from jax import lax
from jax.experimental import pallas as pl
from jax.experimental.pallas import tpu as pltpu

D_MODEL = 1024
NSA_HEADS = 4
NSA_HD = 64
CMP_STRIDE = 16
CMP_BLOCK = 32
CMP_HID = 128
SEL_BLOCK = 64
SEL_TOP_N = 16
WINDOW = 512
POOL_WINDOWS = (2, 4, 8, 16)
POOL_GROUP = 64
POOL_W = 256
POOL_KEEP = 15
CONV_W = 256
DIFF_HEADS = 4
DIFF_QK = 32
DIFF_V = 64
N_BRANCH = 4
BRANCH_W = 256
D_FF = 2816
QBLOCK = 128
PAGE_SIZE = 128
EPS = 1e-6
KV_W = 768
WIN_W = 128
GA_W = 12

P_KV, P_CV, P_QA, P_UB, P_QD, P_WIN, P_GA, P_W = 0, 768, 1536, 1792, 2048, 2304, 2432, 2560
KVB_W = KV_W + WIN_W
R_CV, R_QA, R_UB, R_QD, R_WIN, R_GA, R_W = [p - KV_W for p in (P_CV, P_QA, P_UB, P_QD, P_WIN, P_GA, P_W)]

LANES = 128
SUBLANES = 8
VMEM_LIMIT = 56 * 1024 * 1024

NEG = -1e30
BIG = 1e30
LOG2E = math.log2(math.e)
BF = jnp.bfloat16
F32 = jnp.float32
HI = lax.Precision.HIGHEST

ROW_TILE = 512
KEY_CHUNK = 256
FF_CHUNK = 2816
SAMPLE_GROUP = 2
AUX_TERMS = 4
AUX_SEL_ROW = 64
HALO = 16
FFN_HALO = 8

_NT = (((1,), (1,)), ((), ()))


def _dot(a, b, precision=None):
    return jnp.dot(a, b, preferred_element_type=F32, precision=precision)


def _dot_nt(a, b, precision=None):
    return lax.dot_general(a, b, _NT, preferred_element_type=F32, precision=precision)


def _slopes(n):
    return [float(2.0 ** (-8.0 * (k + 1) / n)) for k in range(n)]


def _norm_mod(x, g, scale, shift):
    ms = jnp.mean(x * x, axis=-1, keepdims=True)
    return (x * lax.rsqrt(ms + EPS) * g) * (1.0 + scale) + shift


def _cparams(sem):
    return pltpu.CompilerParams(dimension_semantics=sem, vmem_limit_bytes=VMEM_LIMIT)


def _const_spec(shape):
    nd = len(shape)
    return pl.BlockSpec(shape, lambda *a: (0,) * nd, pipeline_mode=pl.Buffered(1))


def _ada_kernel(c_ref, w_ref, b_ref, o_ref):
    c = c_ref[...]
    a = (c * jax.nn.sigmoid(c)).astype(BF)
    o_ref[0] = _dot(a, w_ref[0].astype(BF)) + b_ref[0]


def _ada_call(c_all, ada_w, ada_b):
    depth = ada_w.shape[0]
    nb = c_all.shape[0]
    return pl.pallas_call(
        _ada_kernel,
        out_shape=jax.ShapeDtypeStruct((depth, nb, 6 * D_MODEL), F32),
        grid=(depth, 6),
        in_specs=[
            pl.BlockSpec((nb, D_MODEL), lambda l, j: (0, 0)),
            pl.BlockSpec((1, D_MODEL, D_MODEL), lambda l, j: (l, 0, j)),
            pl.BlockSpec((1, 1, D_MODEL), lambda l, j: (l, 0, j)),
        ],
        out_specs=pl.BlockSpec((1, nb, D_MODEL), lambda l, j: (l, 0, j)),
        compiler_params=_cparams(("arbitrary", "arbitrary")),
    )(c_all, ada_w, ada_b.reshape(depth, 1, 6 * D_MODEL))


def _proj_kernel(x_ref, sc_ref, sh_ref, g_ref, w_ref, kv_ref, rest_ref, kvb_ref, kvt_ref):
    h = _norm_mod(x_ref[0], g_ref[...], sc_ref[0, 0, 0], sh_ref[0, 0, 0]).astype(BF)
    y = _dot(h, w_ref[...])
    kv_ref[0] = y[:, P_KV:P_KV + KV_W]
    rest_ref[0] = y[:, KV_W:]
    att = jnp.concatenate([y[:, P_KV:P_KV + KV_W], y[:, P_WIN:P_WIN + WIN_W]], axis=1)
    kvb_ref[0] = att.astype(BF)
    kvt_ref[0] = att.T.astype(BF)


def _mod_spec(l, which, rm):
    return pl.BlockSpec((1, 1, 1, rm, D_MODEL), lambda b, i: (l, b, which, 0, 0))


def _proj_call(l, x3, mod, g, w_packed, tm):
    bx, tx, _ = x3.shape
    rm = mod.shape[3]
    return pl.pallas_call(
        _proj_kernel,
        out_shape=(jax.ShapeDtypeStruct((bx, tx, KV_W), F32), jax.ShapeDtypeStruct((bx, tx, R_W), F32),
                   jax.ShapeDtypeStruct((bx, tx, KVB_W), BF), jax.ShapeDtypeStruct((bx, KVB_W, tx), BF)),
        grid=(bx, tx // tm),
        in_specs=[
            pl.BlockSpec((1, tm, D_MODEL), lambda b, i: (b, i, 0)),
            _mod_spec(l, 1, rm), _mod_spec(l, 0, rm),
            _const_spec((1, D_MODEL)),
            _const_spec((D_MODEL, P_W)),
        ],
        out_specs=(pl.BlockSpec((1, tm, KV_W), lambda b, i: (b, i, 0)),
                   pl.BlockSpec((1, tm, R_W), lambda b, i: (b, i, 0)),
                   pl.BlockSpec((1, tm, KVB_W), lambda b, i: (b, i, 0)),
                   pl.BlockSpec((1, KVB_W, tm), lambda b, i: (b, 0, i))),
        compiler_params=_cparams(("arbitrary", "arbitrary")),
    )(x3, mod, mod, g, w_packed)


def _compress_finish(y, yb, w2):
    n = y.shape[0]
    yn = pltpu.roll(y, n - 1, axis=0)
    hid = CMP_HID
    hk = y[:, 0:hid] + yn[:, hid:2 * hid] + (yb[0:1, 0:hid] + yb[1:2, hid:2 * hid])
    hv = y[:, 2 * hid:3 * hid] + yn[:, 3 * hid:4 * hid] + (yb[0:1, 2 * hid:3 * hid] + yb[1:2, 3 * hid:4 * hid])
    act = jnp.concatenate([jax.nn.gelu(hk), jax.nn.gelu(hv)], axis=1).astype(BF)
    return _dot(act, w2)


def _compress_kernel(ch_ref, wc_ref, pe_ref, w2_ref, o_ref):
    y = _dot(ch_ref[0], wc_ref[...])
    yb = _dot(pe_ref[...], wc_ref[...])
    o_ref[0] = _compress_finish(y, yb, w2_ref[...]).astype(BF)


def _compress_call(chunks, wc, pe2, w2):
    b, n_ch, cw = chunks.shape
    return pl.pallas_call(
        _compress_kernel,
        out_shape=jax.ShapeDtypeStruct((b, n_ch, 2 * NSA_HD), BF),
        grid=(b,),
        in_specs=[pl.BlockSpec((1, n_ch, cw), lambda i: (i, 0, 0)),
                  _const_spec(wc.shape), _const_spec(pe2.shape), _const_spec(w2.shape)],
        out_specs=pl.BlockSpec((1, n_ch, 2 * NSA_HD), lambda i: (i, 0, 0)),
        compiler_params=_cparams(("arbitrary",)),
    )(chunks, wc, pe2, w2)


def _lam_value(lam_ref, lam_init):
    lp = lam_ref[0]
    a = jnp.sum(lp[0:1] * lp[1:2], axis=-1, keepdims=True)
    b = jnp.sum(lp[2:3] * lp[3:4], axis=-1, keepdims=True)
    return jnp.exp(a) - jnp.exp(b) + lam_init


def _group_rms(od, gmat, gain, lam_init):
    ms = _dot(od * od, gmat, precision=HI)
    return od * lax.rsqrt(ms + EPS) * gain * (1.0 - lam_init)


def _prompt_attn_kernel(qa_ref, qd_ref, ga_ref, kvb_ref, kvt_ref, kc_ref, kct_ref, ov_ref, aux_ref, qaux_ref,
                        gmat_ref, lam_ref, gain_ref, oa_ref, od_ref, m_sc, l_sc, acc_sc, s_sc, p_sc, a_sc, pre_sc, *,
                        lam_init, top):
    tq = QBLOCK
    kc_len = KEY_CHUNK
    i = pl.program_id(1)
    t0 = i * tq
    n_sel = ov_ref.shape[0]

    t_row = t0 + lax.broadcasted_iota(jnp.int32, (1, LANES), 1)
    feat = lax.broadcasted_iota(jnp.int32, (LANES, LANES), 0)

    qa = qa_ref[0] * (NSA_HD ** -0.5 * LOG2E)
    qa_t = [qa[:, 0:LANES].T, qa[:, LANES:2 * LANES].T]
    slabs = []
    for h in range(NSA_HEADS):
        tile = qa_t[h // 2]
        if h % 2:
            tile = pltpu.roll(tile, NSA_HD, axis=0)
        slabs.append(jnp.where(feat < NSA_HD, tile, 0.0))
    q_nsa = jnp.concatenate(slabs, axis=1).astype(BF)

    qd = qd_ref[0] * (DIFF_QK ** -0.5 * LOG2E)
    q_diff = []
    for g in range(2):
        tile = qd[:, g * LANES:(g + 1) * LANES].T
        slabs = [jnp.where(lax.shift_right_logical(feat, 5) == sl, tile, 0.0) for sl in range(4)]
        q_diff.append(jnp.concatenate(slabs, axis=1).astype(BF))

    def lanes_of(x, sl):
        return x[:, sl * tq:(sl + 1) * tq]

    kc = kc_ref[0]
    n_ch = kc.shape[0]
    s = _dot(kc, q_nsa)
    cmp_end = lax.broadcasted_iota(jnp.int32, (n_ch, LANES), 0) * CMP_STRIDE + (CMP_BLOCK - 1)
    vis = cmp_end <= t_row
    p_heads = []
    for h in range(NSA_HEADS):
        sh = jnp.where(vis, lanes_of(s, h), NEG)
        m = jnp.max(sh, axis=0, keepdims=True)
        e = jnp.where(vis, jnp.exp2(sh - m), 0.0)
        d = jnp.sum(e, axis=0, keepdims=True)
        p_heads.append(e / jnp.where(d > 0, d, 1.0))
    o_cmp = _dot(kct_ref[0], jnp.concatenate(p_heads, axis=1).astype(BF))
    p_sum = p_heads[0] + p_heads[1] + p_heads[2] + p_heads[3]
    imp_t = _dot(ov_ref[...], p_sum, precision=HI)

    j_full = lax.broadcasted_iota(jnp.int32, (n_sel, LANES), 0)
    valid = j_full * SEL_BLOCK <= t_row
    cur = lax.shift_right_logical(t_row, 6)
    forced = jnp.where(j_full == 0, 1, jnp.where(j_full == cur, 1, jnp.where(j_full == cur - 1, 1, 0)))
    score = jnp.where(valid, jnp.where(forced > 0, BIG, imp_t), -BIG)
    groups = [score[SUBLANES * v:SUBLANES * (v + 1)] for v in range(n_sel // SUBLANES)]
    counts = [jnp.zeros((SUBLANES, LANES), F32) for _ in groups]
    j_loc = lax.broadcasted_iota(jnp.int32, (SUBLANES, LANES), 0)
    for ii in range(n_sel):
        si = score[ii:ii + 1, :]
        for v, sv in enumerate(groups):
            ge = jnp.where(si >= sv, 1.0, 0.0)
            gt = jnp.where(si > sv, 1.0, 0.0)
            if v > ii // SUBLANES:
                inc = ge
            elif v < ii // SUBLANES:
                inc = gt
            else:
                inc = jnp.where(j_loc > ii % SUBLANES, ge, gt)
            counts[v] = counts[v] + inc
    rank = jnp.concatenate(counts, axis=0)
    sel_neg = jnp.where(rank < top, 0.0, NEG)

    m_sc[...] = jnp.full(m_sc.shape, NEG, F32)
    l_sc[...] = jnp.zeros(l_sc.shape, F32)
    acc_sc[...] = jnp.zeros(acc_sc.shape, F32)

    aux_nsa = qaux_ref[0]
    pad = [jnp.zeros((AUX_SEL_ROW - n_sel, 4 * tq), F32)] if n_sel < AUX_SEL_ROW else []
    aux_sel = jnp.concatenate([aux_nsa[0:AUX_SEL_ROW], jnp.concatenate([sel_neg] * 4, axis=1)] + pad, axis=0)
    q_win = jnp.concatenate([q_nsa, aux_nsa.astype(BF)], axis=0)
    k_tile = [1, 2, 3]
    v_tile = [1, 4, 5]
    q_grp = [jnp.concatenate([q_nsa, aux_sel.astype(BF)], axis=0),
             jnp.concatenate([q_diff[0], qaux_ref[1].astype(BF)], axis=0),
             jnp.concatenate([q_diff[1], qaux_ref[2].astype(BF)], axis=0)]
    n_full = t0 // kc_len

    def scores(c, buf):
        ks = pl.multiple_of(c * kc_len, kc_len)
        aux_k = aux_ref[pl.ds(ks, kc_len), :]
        for grp in range(3):
            k_op = jnp.concatenate(
                [kvb_ref[0, pl.ds(ks, kc_len), k_tile[grp] * LANES:(k_tile[grp] + 1) * LANES], aux_k], axis=1)
            s_sc[buf, grp] = _dot(k_op, q_grp[grp])

    def softmax(c, buf, diagonal):
        if diagonal:
            pos = c * kc_len + lax.broadcasted_iota(jnp.int32, (kc_len, LANES), 0)
            causal_bias = jnp.where(pos <= t_row, 0.0, NEG)
        for grp in range(3):
            m_old_all = m_sc[grp]
            l_old_all = l_sc[grp]
            alphas, ms, ls = [], [], []
            for sl in range(4):
                s_h = s_sc[buf, grp, :, sl * tq:(sl + 1) * tq]
                if diagonal:
                    s_h = s_h + causal_bias
                m_old = lanes_of(m_old_all, sl)
                m_new = jnp.maximum(m_old, jnp.max(s_h, axis=0, keepdims=True))
                alpha = jnp.exp2(m_old - m_new)
                e = jnp.exp2(s_h - m_new)
                ls.append(alpha * lanes_of(l_old_all, sl) + jnp.sum(e, axis=0, keepdims=True))
                ms.append(m_new)
                alphas.append(alpha)
                p_sc[buf, grp, :, sl * tq:(sl + 1) * tq] = e.astype(BF)
            m_sc[grp] = jnp.concatenate(ms, axis=1)
            l_sc[grp] = jnp.concatenate(ls, axis=1)
            a_sc[buf, grp] = jnp.concatenate(alphas, axis=1)

    def values(c, buf):
        ks = pl.multiple_of(jnp.maximum(c, 0) * kc_len, kc_len)
        for grp in range(3):
            v_t = kvt_ref[0, v_tile[grp] * LANES:(v_tile[grp] + 1) * LANES, pl.ds(ks, kc_len)]
            acc_sc[grp] = a_sc[buf, grp] * acc_sc[grp] + _dot(v_t, p_sc[buf, grp])

    def stage(c, buf):
        scores(c + 1, 1 - buf)
        values(c - 1, 1 - buf)
        softmax(c, buf, False)

    def last(c, buf):
        values(c - 1, 1 - buf)
        softmax(c, buf, True)
        values(c, buf)

    span = WINDOW + tq
    start = pl.multiple_of(jnp.maximum(i - WINDOW // tq, 0) * tq, tq)
    pos_w = start + lax.broadcasted_iota(jnp.int32, (span, LANES), 0)
    dist = t_row - pos_w
    mb_w = jnp.where(dist >= 0, jnp.where(dist < WINDOW, 0.0, NEG), NEG)
    k_win = jnp.concatenate([kvb_ref[0, pl.ds(start, span), KV_W:KV_W + WIN_W], aux_ref[pl.ds(start, span), :]],
                            axis=1)
    sw = _dot(k_win, q_win)
    pw, inv_w = [], []
    for h in range(NSA_HEADS):
        s_h = lanes_of(sw, h) + mb_w
        e = jnp.exp2(s_h - jnp.max(s_h, axis=0, keepdims=True))
        inv_w.append(1.0 / jnp.sum(e, axis=0, keepdims=True))
        pw.append(e.astype(BF))
    o_win = (_dot(kvt_ref[0, KV_W:KV_W + WIN_W, pl.ds(start, span)], jnp.concatenate(pw, axis=1))
             * jnp.concatenate(inv_w, axis=1))
    ga_t = jax.nn.sigmoid(ga_ref[0]).T
    pre_sc[...] = jnp.concatenate(
        [ga_t[3 * h:3 * h + 1] * lanes_of(o_cmp, h) + ga_t[3 * h + 2:3 * h + 3] * lanes_of(o_win, h)
         for h in range(NSA_HEADS)], axis=1)

    scores(0, 0)
    p_sc[1] = jnp.zeros(p_sc.shape[1:], BF)
    a_sc[1] = jnp.ones(a_sc.shape[1:], F32)
    n_pairs = n_full // 2

    def pair(k, carry):
        stage(2 * k, 0)
        stage(2 * k + 1, 1)
        return carry

    lax.fori_loop(0, n_pairs, pair, 0)

    @pl.when(n_full == 2 * n_pairs)
    def _():
        last(n_full, 0)

    @pl.when(n_full != 2 * n_pairs)
    def _():
        stage(n_full - 1, 0)
        last(n_full, 1)

    o_sel = acc_sc[0] / l_sc[0]

    pre = pre_sc[...]
    oa_h = [lanes_of(pre, h) + ga_t[3 * h + 1:3 * h + 2] * lanes_of(o_sel, h) for h in range(NSA_HEADS)]
    oa_ref[0] = jnp.concatenate(
        [jnp.concatenate([oa_h[0][NSA_HD:], oa_h[1][NSA_HD:]], axis=0).T,
         jnp.concatenate([oa_h[2][NSA_HD:], oa_h[3][NSA_HD:]], axis=0).T], axis=1)

    lam = _lam_value(lam_ref, lam_init)
    tiles = []
    for g in range(2):
        on = acc_sc[1 + g] / l_sc[1 + g]
        a0 = lanes_of(on, 0) - lam * lanes_of(on, 1)
        a1 = lanes_of(on, 2) - lam * lanes_of(on, 3)
        tiles.append(jnp.concatenate([a0[:DIFF_V], a1[DIFF_V:]], axis=0).T)
    od = jnp.concatenate(tiles, axis=1)
    od_ref[0] = _group_rms(od, gmat_ref[...], gain_ref[0], lam_init)


def _prompt_attn_call(l, proj, kvb, kvt, kcvc, kct, ov, aux, qaux, gmat, diff_lam, gain, lam_init):
    b, t, _ = proj.shape
    n_ch = kcvc.shape[1]
    n_sel = ov.shape[0]
    tq = QBLOCK
    assert n_sel <= LANES - AUX_SEL_ROW and KEY_CHUNK % tq == 0
    kern = functools.partial(_prompt_attn_kernel, lam_init=lam_init, top=min(SEL_TOP_N, n_sel))
    return pl.pallas_call(
        kern,
        out_shape=(jax.ShapeDtypeStruct((b, t, BRANCH_W), F32), jax.ShapeDtypeStruct((b, t, BRANCH_W), F32)),
        grid=(b, t // tq),
        in_specs=[
            pl.BlockSpec((1, tq, 256), lambda bi, i: (bi, i, R_QA // 256)),
            pl.BlockSpec((1, tq, 256), lambda bi, i: (bi, i, R_QD // 256)),
            pl.BlockSpec((1, tq, LANES), lambda bi, i: (bi, i, R_GA // LANES)),
            pl.BlockSpec((1, t, KVB_W), lambda bi, i: (bi, 0, 0)),
            pl.BlockSpec((1, KVB_W, t), lambda bi, i: (bi, 0, 0)),
            pl.BlockSpec((1, n_ch, 2 * NSA_HD), lambda bi, i: (bi, 0, 0)),
            pl.BlockSpec((1, 2 * NSA_HD, n_ch), lambda bi, i: (bi, 0, 0)),
            _const_spec(ov.shape), _const_spec(aux.shape), _const_spec(qaux.shape), _const_spec(gmat.shape),
            pl.BlockSpec((1, 4, DIFF_QK), lambda bi, i: (l, 0, 0)),
            pl.BlockSpec((1, 1, BRANCH_W), lambda bi, i: (l, 0, 0)),
        ],
        out_specs=(pl.BlockSpec((1, tq, BRANCH_W), lambda bi, i: (bi, i, 0)),
                   pl.BlockSpec((1, tq, BRANCH_W), lambda bi, i: (bi, i, 0))),
        scratch_shapes=[pltpu.VMEM((3, 1, 4 * tq), F32), pltpu.VMEM((3, 1, 4 * tq), F32),
                        pltpu.VMEM((3, LANES, 4 * tq), F32),
                        pltpu.VMEM((2, 3, KEY_CHUNK, 4 * tq), F32), pltpu.VMEM((2, 3, KEY_CHUNK, 4 * tq), BF),
                        pltpu.VMEM((2, 3, 1, 4 * tq), F32), pltpu.VMEM((LANES, 4 * tq), F32)],
        compiler_params=_cparams(("arbitrary", "arbitrary")),
    )(proj, proj, proj, kvb, kvt, kcvc, kct, ov, aux, qaux, gmat, diff_lam, gain)


def _sample_attn_seq(kvn_ref, rest_ref, *refs, n_pages, lam_init, top, n_sel):
    pages = refs[:n_pages]
    (swin_ref, wc_ref, pe_ref, w2_ref, ov_ref, e_ref, gmat_ref, lam_ref, gain_ref, oa_ref, od_ref,
     cmp_rows) = refs[n_pages:]
    past = n_pages * PAGE_SIZE
    n_chp = past // CMP_STRIDE
    n_vis = (past - (CMP_BLOCK - 1)) // CMP_STRIDE + 1
    cur = past // SEL_BLOCK
    nsa_slopes = _slopes(NSA_HEADS)
    diff_slopes = _slopes(DIFF_HEADS)

    row = jnp.concatenate([kvn_ref[0], rest_ref[0]], axis=1)
    kv_new = row[:, P_KV:P_KV + KV_W]
    lane = lax.broadcasted_iota(jnp.int32, (1, LANES), 1)
    lo = lane < NSA_HD
    rid = lax.broadcasted_iota(jnp.int32, (SUBLANES, 1), 0)
    rid_full = lax.broadcasted_iota(jnp.int32, (SUBLANES, LANES), 0)

    def bcast(v):
        return jnp.broadcast_to(v, (SUBLANES, v.shape[1]))

    qa = row[:, P_QA:P_QA + 256] * (NSA_HD ** -0.5)
    b0, b1 = bcast(qa[:, 0:LANES]), bcast(qa[:, LANES:2 * LANES])
    cand = [b0, pltpu.roll(b0, NSA_HD, axis=1), b1, pltpu.roll(b1, NSA_HD, axis=1)]
    q8 = jnp.zeros((SUBLANES, LANES), F32)
    for h in range(NSA_HEADS):
        q8 = jnp.where(rid_full == h, cand[h], q8)
    q8 = jnp.where(lo, q8, 0.0)
    q8b = q8.astype(BF)
    slope_col = jnp.zeros((SUBLANES, 1), F32)
    for h in range(NSA_HEADS):
        slope_col = jnp.where(rid == h, nsa_slopes[h], slope_col)

    for j, pg in enumerate(pages):
        cmp_rows[j * PAGE_SIZE:(j + 1) * PAGE_SIZE, :] = pg[0, 0, :, 0:LANES]
    chunks = jnp.concatenate([cmp_rows[pl.ds(r, n_chp, stride=CMP_STRIDE), :].astype(BF)
                              for r in range(CMP_STRIDE)], axis=1)
    y = yield chunks
    yb = _dot(pe_ref[...], wc_ref[...])
    kc = _compress_finish(y, yb, w2_ref[...]).astype(BF)
    yield

    cl = lax.broadcasted_iota(jnp.int32, (1, n_chp), 1)
    vis = cl < n_vis
    s = jnp.where(vis, _dot_nt(q8b, kc), NEG)
    m = jnp.max(s, axis=-1, keepdims=True)
    e = jnp.where(vis, jnp.exp(s - m), 0.0)
    d = jnp.sum(e, axis=-1, keepdims=True)
    p_c = jnp.where(rid < NSA_HEADS, e / jnp.where(d > 0, d, 1.0), 0.0)
    o_cmp = _dot(p_c.astype(BF), kc)
    imp = jnp.sum(_dot(p_c, ov_ref[...], precision=HI), axis=0, keepdims=True)
    yield

    forced = jnp.where(lane == 0, 1, jnp.where(lane == cur, 1, jnp.where(lane == cur - 1, 1, 0)))
    score = jnp.where(lane < n_sel, jnp.where(forced > 0, BIG, imp), -BIG)
    r_row = jnp.broadcast_to(score, (LANES, LANES))
    r_col = r_row.T
    ii = lax.broadcasted_iota(jnp.int32, (LANES, LANES), 0)
    jj = lax.broadcasted_iota(jnp.int32, (LANES, LANES), 1)
    inc = jnp.where(r_col > r_row, 1.0, jnp.where(r_col == r_row, jnp.where(ii < jj, 1.0, 0.0), 0.0))
    rank = jnp.sum(inc, axis=0, keepdims=True)
    sel = jnp.where(rank < top, 1.0, 0.0)
    sel_exp = _dot(bcast(sel).astype(BF), e_ref[...])
    yield

    pos = lax.broadcasted_iota(jnp.int32, (1, past), 1)
    rel = (pos - past).astype(F32)

    def attend_new(q, sc, k_tile, kv_tile_new):
        s_new = jnp.sum(q * kv_tile_new, axis=-1, keepdims=True)
        mm = jnp.maximum(jnp.max(sc, axis=-1, keepdims=True), s_new)
        ee = jnp.exp(sc - mm)
        e_new = jnp.exp(s_new - mm)
        den = jnp.sum(ee, axis=-1, keepdims=True) + e_new
        return (_dot(ee.astype(BF), k_tile) + e_new * kv_tile_new) / den

    k_sel = jnp.concatenate([pg[0, 0, :, LANES:2 * LANES] for pg in pages], axis=0).astype(BF)
    sc = _dot_nt(q8b, k_sel) + slope_col * rel + jnp.where(sel_exp > 0.5, 0.0, NEG)
    o_sel = attend_new(q8, sc, k_sel, kv_new[:, LANES:2 * LANES])
    yield

    k_win = swin_ref[0, 0].astype(BF)
    nw = k_win.shape[0]
    wl = lax.broadcasted_iota(jnp.int32, (1, nw), 1)
    dist_w = nw - wl
    sc = _dot_nt(q8b, k_win) - slope_col * dist_w.astype(F32) + jnp.where(dist_w < WINDOW, 0.0, NEG)
    o_win = attend_new(q8, sc, k_win, row[:, P_WIN:P_WIN + WIN_W])

    ga8 = bcast(jax.nn.sigmoid(row[:, P_GA:P_GA + LANES]))
    gates = [jnp.sum(jnp.where(lane == 3 * rid_full + k, ga8, 0.0), axis=-1, keepdims=True) for k in range(3)]
    oa8 = gates[0] * o_cmp + gates[1] * o_sel + gates[2] * o_win
    oa_ref[0] = jnp.concatenate(
        [jnp.where(lo, pltpu.roll(oa8[0:1], NSA_HD, axis=1), oa8[1:2]),
         jnp.where(lo, pltpu.roll(oa8[2:3], NSA_HD, axis=1), oa8[3:4])], axis=1)
    yield

    qd = bcast(row[:, P_QD:P_QD + 256] * (DIFF_QK ** -0.5))
    lane2 = lax.broadcasted_iota(jnp.int32, (SUBLANES, 2 * LANES), 1)
    rid2 = lax.broadcasted_iota(jnp.int32, (SUBLANES, 2 * LANES), 0)
    qd8 = jnp.where(lax.shift_right_logical(lane2, 5) == rid2, qd, 0.0)
    dslope = jnp.zeros((SUBLANES, 1), F32)
    for r in range(SUBLANES):
        dslope = jnp.where(rid == r, diff_slopes[r // 2], dslope)
    k_d = jnp.concatenate([pg[0, 0, :, 2 * LANES:4 * LANES] for pg in pages], axis=0).astype(BF)
    v_d = jnp.concatenate([pg[0, 0, :, 4 * LANES:6 * LANES] for pg in pages], axis=0).astype(BF)
    sc = _dot_nt(qd8.astype(BF), k_d) + dslope * rel
    s_new = jnp.sum(qd8 * kv_new[:, 2 * LANES:4 * LANES], axis=-1, keepdims=True)
    mm = jnp.maximum(jnp.max(sc, axis=-1, keepdims=True), s_new)
    ee = jnp.exp(sc - mm)
    e_new = jnp.exp(s_new - mm)
    den = jnp.sum(ee, axis=-1, keepdims=True) + e_new
    o8 = (_dot(ee.astype(BF), v_d) + e_new * kv_new[:, 4 * LANES:6 * LANES]) / den
    lam = _lam_value(lam_ref, lam_init)
    coef = jnp.where((rid2 & 1) == 0, 1.0, -lam)
    head_mask = lax.shift_right_logical(lane2, 6) == lax.shift_right_logical(rid2, 1)
    od = jnp.sum(jnp.where(head_mask, coef * o8, 0.0), axis=0, keepdims=True)
    od8 = _group_rms(bcast(od), gmat_ref[...], gain_ref[0], lam_init)
    od_ref[0] = od8[0:1]


def _sample_group_kernel(pt_ref, kvn_ref, rest_ref, *refs, n_pages, group, **kw):
    pages = refs[:group * n_pages]
    swin_ref, *consts, oa_ref, od_ref, cmp_rows = refs[group * n_pages:]
    seqs = []
    for s in range(group):
        one = pl.ds(s, 1)
        seqs.append(_sample_attn_seq(kvn_ref.at[one], rest_ref.at[one],
                                     *pages[s * n_pages:(s + 1) * n_pages], swin_ref.at[:, one],
                                     *consts, oa_ref.at[one], od_ref.at[one], cmp_rows.at[s], n_pages=n_pages, **kw))
    chunks = [next(seq) for seq in seqs]
    n_chp = chunks[0].shape[0]
    y_all = _dot(jnp.concatenate(chunks, axis=0), consts[0][...])
    for s, seq in enumerate(seqs):
        seq.send(y_all[s * n_chp:(s + 1) * n_chp])
    live = True
    while live:
        live = False
        for seq in seqs:
            live = next(seq, "done") != "done" or live


def _sample_attn_call(l, page_table, kv_new, rest, cache_kv, state_win, wc, pe2, w2, ov, emat, gmat, diff_lam, gain,
                      lam_init, n_sel):
    bs = rest.shape[0]
    n_pages = page_table.shape[1]
    nw = state_win.shape[2]
    group = SAMPLE_GROUP if bs % SAMPLE_GROUP == 0 else 1
    kern = functools.partial(_sample_group_kernel, n_pages=n_pages, group=group, lam_init=lam_init,
                             top=min(SEL_TOP_N, n_sel), n_sel=n_sel)

    def page_spec(s, j):
        return pl.BlockSpec((1, 1, PAGE_SIZE, KV_W), lambda b, pt: (l, pt[group * b + s, j], 0, 0))

    def cspec(shape):
        nd = len(shape)
        return pl.BlockSpec(shape, lambda b, pt: (0,) * nd, pipeline_mode=pl.Buffered(1))

    grid_spec = pltpu.PrefetchScalarGridSpec(
        num_scalar_prefetch=1,
        grid=(bs // group,),
        in_specs=[pl.BlockSpec((group, 1, KV_W), lambda b, pt: (b, 0, 0)),
                  pl.BlockSpec((group, 1, R_W), lambda b, pt: (b, 0, 0))]
        + [page_spec(s, j) for s in range(group) for j in range(n_pages)]
        + [pl.BlockSpec((1, group, nw, WIN_W), lambda b, pt: (l, b, 0, 0)),
           cspec(wc.shape), cspec(pe2.shape), cspec(w2.shape), cspec(ov.shape), cspec(emat.shape),
           cspec(gmat.shape),
           pl.BlockSpec((1, 4, DIFF_QK), lambda b, pt: (l, 0, 0)),
           pl.BlockSpec((1, 1, BRANCH_W), lambda b, pt: (l, 0, 0))],
        out_specs=(pl.BlockSpec((group, 1, BRANCH_W), lambda b, pt: (b, 0, 0)),
                   pl.BlockSpec((group, 1, BRANCH_W), lambda b, pt: (b, 0, 0))),
        scratch_shapes=[pltpu.VMEM((group, n_pages * PAGE_SIZE, LANES), F32)],
    )
    return pl.pallas_call(
        kern,
        out_shape=(jax.ShapeDtypeStruct((bs, 1, BRANCH_W), F32), jax.ShapeDtypeStruct((bs, 1, BRANCH_W), F32)),
        grid_spec=grid_spec,
        compiler_params=_cparams(("arbitrary",)),
    )(page_table, kv_new, rest, *([cache_kv] * (group * n_pages)), state_win, wc, pe2, w2, ov, emat, gmat, diff_lam, gain)


def _merge_kernel(*refs, halo):
    if halo:
        (x_ref, sc_ref, sh_ref, gt_ref, g_ref, cv_ref, ub_ref, cvp_ref, ubp_ref, oa_ref, od_ref,
         wg_ref, wb_ref, wo_ref, pw_ref, ps_ref, cw_ref, xo_ref, zin_ref) = refs
    else:
        (x_ref, sc_ref, sh_ref, gt_ref, g_ref, cv_ref, ub_ref, stc_ref, stp_ref, oa_ref, od_ref,
         wg_ref, wb_ref, wo_ref, pw_ref, ps_ref, cw_ref, xo_ref, zin_ref) = refs
    i = pl.program_id(1)
    x = x_ref[0]
    tm = x.shape[0]
    h = _norm_mod(x, g_ref[...], sc_ref[0, 0, 0], sh_ref[0, 0, 0]).astype(BF)

    cv = cv_ref[0]
    u = ub_ref[0]
    bg = cv[:, 0:CONV_W]
    zin = cv[:, CONV_W:2 * CONV_W] * cv[:, 2 * CONV_W:3 * CONV_W]
    zin_ref[0] = zin
    cw = cw_ref[...]
    lane = lax.broadcasted_iota(jnp.int32, (1, POOL_W), 1)
    win_lane = jnp.where(lane < 64, 2, jnp.where(lane < 128, 4, jnp.where(lane < 192, 8, 16)))

    if halo:
        keep = jnp.where(i > 0, 1.0, 0.0)
        cvp = cvp_ref[0] * keep
        ubp = ubp_ref[0] * keep
        z_ext = jnp.concatenate([cvp[:, CONV_W:2 * CONV_W] * cvp[:, 2 * CONV_W:3 * CONV_W], zin], axis=0)
        z = (cw[0:1] * pltpu.roll(z_ext, 2, axis=0) + cw[1:2] * pltpu.roll(z_ext, 1, axis=0)
             + cw[2:3] * z_ext)[HALO:]
        ext = jnp.concatenate([ubp, u], axis=0)
        s2 = ext + pltpu.roll(ext, 1, axis=0)
        s4 = s2 + pltpu.roll(s2, 2, axis=0)
        s8 = s4 + pltpu.roll(s4, 4, axis=0)
        s16 = s8 + pltpu.roll(s8, 8, axis=0)
        tot = jnp.where(lane < 64, s2, jnp.where(lane < 128, s4, jnp.where(lane < 192, s8, s16)))[HALO:]
        pos1 = i * tm + lax.broadcasted_iota(jnp.int32, (tm, 1), 0) + 1
        cnt = jnp.minimum(win_lane, pos1).astype(F32)
    else:
        z = cw[0:1] * stc_ref[0] + cw[1:2] * stc_ref[1] + cw[2:3] * zin
        s2 = u + stp_ref[POOL_KEEP - 1]
        s4 = s2 + stp_ref[POOL_KEEP - 2] + stp_ref[POOL_KEEP - 3]
        s8 = s4
        for k in range(4, 8):
            s8 = s8 + stp_ref[POOL_KEEP - k]
        s16 = s8
        for k in range(8, 16):
            s16 = s16 + stp_ref[POOL_KEEP - k]
        tot = jnp.where(lane < 64, s2, jnp.where(lane < 128, s4, jnp.where(lane < 192, s8, s16)))
        cnt = win_lane.astype(F32)
    o_c = bg * z
    pooled = tot / cnt - u
    o_b = _dot(pooled.astype(BF), pw_ref[...]) * ps_ref[...]

    branches = [oa_ref[0], o_b, o_c, od_ref[0]]
    mixed = jnp.zeros((tm, D_MODEL), F32)
    for n in range(N_BRANCH):
        pb = _dot(branches[n].astype(BF), wb_ref[n])
        gate = jax.nn.sigmoid(_dot(h, wg_ref[:, n * D_MODEL:(n + 1) * D_MODEL]))
        mixed = mixed + gate * pb
    xo_ref[0] = x + gt_ref[0, 0, 0] * _dot(mixed.astype(BF), wo_ref[...])


def _merge_call(l, x3, mod, g, proj, prev_c, prev_p, o_a, o_d, wg, wb, wo, pw, ps, cw, tm, halo):
    bx, tx, _ = x3.shape
    rm = mod.shape[3]
    if halo:
        nblk = tm // HALO
        prev_specs = [
            pl.BlockSpec((1, HALO, 3 * CONV_W), lambda b, i: (b, jnp.maximum(i * nblk - 1, 0), R_CV // (3 * CONV_W))),
            pl.BlockSpec((1, HALO, POOL_W), lambda b, i: (b, jnp.maximum(i * nblk - 1, 0), R_UB // POOL_W)),
        ]
    else:
        prev_specs = [pl.BlockSpec(prev_c.shape, lambda b, i: (0, 0, 0)),
                      pl.BlockSpec(prev_p.shape, lambda b, i: (0, 0, 0))]
    return pl.pallas_call(
        functools.partial(_merge_kernel, halo=halo),
        out_shape=(jax.ShapeDtypeStruct((bx, tx, D_MODEL), F32), jax.ShapeDtypeStruct((bx, tx, CONV_W), F32)),
        grid=(bx, tx // tm),
        in_specs=[
            pl.BlockSpec((1, tm, D_MODEL), lambda b, i: (b, i, 0)),
            _mod_spec(l, 1, rm), _mod_spec(l, 0, rm), _mod_spec(l, 2, rm),
            _const_spec((1, D_MODEL)),
            pl.BlockSpec((1, tm, 3 * CONV_W), lambda b, i: (b, i, R_CV // (3 * CONV_W))),
            pl.BlockSpec((1, tm, POOL_W), lambda b, i: (b, i, R_UB // POOL_W)),
            *prev_specs,
            pl.BlockSpec((1, tm, BRANCH_W), lambda b, i: (b, i, 0)),
            pl.BlockSpec((1, tm, BRANCH_W), lambda b, i: (b, i, 0)),
            _const_spec(wg.shape), _const_spec(wb.shape), _const_spec(wo.shape),
            _const_spec(pw.shape), _const_spec(ps.shape), _const_spec(cw.shape),
        ],
        out_specs=(pl.BlockSpec((1, tm, D_MODEL), lambda b, i: (b, i, 0)),
                   pl.BlockSpec((1, tm, CONV_W), lambda b, i: (b, i, 0))),
        compiler_params=_cparams(("arbitrary", "arbitrary")),
    )(x3, mod, mod, mod, g, proj, proj, prev_c, prev_p, o_a, o_d, wg, wb, wo, pw, ps, cw)


def _ffn_kernel(*refs, halo, final, keep_rows):
    (x_ref, sc_ref, sh_ref, gt_ref, g_ref, prev_ref, wu_ref, fc_ref, wd_ref, fg_ref, xo_ref, up_ref) = refs
    i = pl.program_id(1)
    x = x_ref[0]
    tm = x.shape[0]
    g = g_ref[...]
    sc = sc_ref[0, 0, 0]
    sh = sh_ref[0, 0, 0]
    if halo:
        hx = jnp.concatenate([prev_ref[0], x], axis=0)
        h2 = _norm_mod(hx, g, sc, sh).astype(BF)
        rows = lax.broadcasted_iota(jnp.int32, (tm + FFN_HALO, 1), 0)
        live = jnp.where(rows >= FFN_HALO, 1.0, jnp.where(i > 0, 1.0, 0.0))
    else:
        h2 = _norm_mod(x, g, sc, sh).astype(BF)
    acc = jnp.zeros((tm, D_MODEL), F32)
    for c in range(D_FF // FF_CHUNK):
        halves = []
        for part in range(2):
            col = part * D_FF + c * FF_CHUNK
            up = _dot(h2, wu_ref[:, col:col + FF_CHUNK])
            w = fc_ref[:, col:col + FF_CHUNK]
            if halo:
                up = up * live
                up_ref[0, :, col:col + FF_CHUNK] = up[FFN_HALO + tm - keep_rows:]
                upc = (w[0:1] * pltpu.roll(up, 2, axis=0) + w[1:2] * pltpu.roll(up, 1, axis=0)
                       + w[2:3] * up)[FFN_HALO:]
            else:
                up_ref[0, :, col:col + FF_CHUNK] = up
                upc = (w[0:1] * prev_ref[0, :, col:col + FF_CHUNK] + w[1:2] * prev_ref[1, :, col:col + FF_CHUNK]
                       + w[2:3] * up)
            halves.append(upc)
        val, gg = halves
        act = (gg * jax.nn.sigmoid(gg) * val).astype(BF)
        acc = acc + _dot(act, wd_ref[c * FF_CHUNK:(c + 1) * FF_CHUNK, :])
    out = x + gt_ref[0, 0, 0] * acc
    if final:
        ms = jnp.mean(out * out, axis=-1, keepdims=True)
        out = out * lax.rsqrt(ms + EPS) * fg_ref[...]
    xo_ref[0] = out


def _ffn_call(l, x3, mod, g, prev, wu, fc, wd, fg, tm, halo, final):
    bx, tx, _ = x3.shape
    rm = mod.shape[3]
    keep_rows = SUBLANES if halo else tm
    if halo:
        nblk = tm // FFN_HALO
        prev_spec = pl.BlockSpec((1, FFN_HALO, D_MODEL), lambda b, i: (b, jnp.maximum(i * nblk - 1, 0), 0))
    else:
        prev_spec = pl.BlockSpec(prev.shape, lambda b, i: (0, 0, 0))
    return pl.pallas_call(
        functools.partial(_ffn_kernel, halo=halo, final=final, keep_rows=keep_rows),
        out_shape=(jax.ShapeDtypeStruct((bx, tx, D_MODEL), F32),
                   jax.ShapeDtypeStruct((bx, keep_rows, 2 * D_FF), F32)),
        grid=(bx, tx // tm),
        in_specs=[
            pl.BlockSpec((1, tm, D_MODEL), lambda b, i: (b, i, 0)),
            _mod_spec(l, 4, rm), _mod_spec(l, 3, rm), _mod_spec(l, 5, rm),
            _const_spec((1, D_MODEL)),
            prev_spec,
            _const_spec(wu.shape), _const_spec(fc.shape), _const_spec(wd.shape), _const_spec((1, D_MODEL)),
        ],
        out_specs=(pl.BlockSpec((1, tm, D_MODEL), lambda b, i: (b, i, 0)),
                   pl.BlockSpec((1, keep_rows, 2 * D_FF), lambda b, i: (b, 0, 0))),
        compiler_params=_cparams(("arbitrary", "arbitrary")),
    )(x3, mod, mod, mod, g, prev, wu, fc, wd, fg)


def _pack_w_in(w_in):
    o = np.cumsum([0, 256, KV_W, WIN_W, GA_W, POOL_W, 3 * CONV_W, 256, N_BRANCH * D_MODEL])
    q_a, kv, win, g_a, u_b, cv, q_d, gates = [w_in[..., o[k]:o[k + 1]] for k in range(8)]
    g_a = jnp.pad(g_a, ((0, 0), (0, 0), (0, LANES - GA_W)))
    packed = jnp.concatenate([kv, cv, q_a, u_b, q_d, win, g_a], axis=-1).astype(BF)
    return packed, gates.astype(BF)


def _compress_weights(cmp_pe, cmp_w1, cmp_w2):
    depth = cmp_w1.shape[0]
    half = CMP_STRIDE * NSA_HD
    w1 = cmp_w1.reshape(depth, 2, 2, CMP_STRIDE, NSA_HD, CMP_HID)
    z = jnp.zeros((depth, CMP_STRIDE, NSA_HD, CMP_HID), cmp_w1.dtype)
    cols = []
    for kv in range(2):
        for hf in range(2):
            blk = w1[:, kv, hf]
            pair = (blk, z) if kv == 0 else (z, blk)
            cols.append(jnp.concatenate(pair, axis=2).reshape(depth, 2 * half, CMP_HID))
    wc = jnp.concatenate(cols, axis=-1).astype(BF)
    pe = cmp_pe.reshape(depth, 2, 2, CMP_STRIDE, NSA_HD)
    pe2 = jnp.concatenate([pe[:, 0], pe[:, 1]], axis=-1).reshape(depth, 2, 2 * half)
    pe2 = jnp.pad(pe2, ((0, 0), (0, SUBLANES - 2), (0, 0))).astype(BF)
    zz = jnp.zeros((depth, CMP_HID, NSA_HD), cmp_w2.dtype)
    w2 = jnp.concatenate([jnp.concatenate([cmp_w2[:, 0], zz], axis=-1),
                          jnp.concatenate([zz, cmp_w2[:, 1]], axis=-1)], axis=1).astype(BF)
    return wc, pe2, w2


def _overlap(n_cmp, n_sel):
    cs = np.arange(n_cmp)[:, None] * CMP_STRIDE
    ss = np.arange(n_sel)[None, :] * SEL_BLOCK
    ov = np.minimum(cs + CMP_BLOCK, ss + SEL_BLOCK) - np.maximum(cs, ss)
    return (np.maximum(ov, 0) / CMP_BLOCK).astype(np.float32)


def _expand_matrix(n_rows, n_pos):
    return (np.arange(n_pos)[None, :] // SEL_BLOCK == np.arange(n_rows)[:, None]).astype(np.float32)


def _bf16_terms(x, n):
    terms = []
    for _ in range(n):
        u = np.array(x, np.float32).view(np.uint32)
        t = ((u + ((u >> 16) & 1) + 0x7FFF) & np.uint32(0xFFFF0000)).view(np.float32)
        terms.append(float(t))
        x = x - float(t)
    return terms


def _aux_key_table(n_pos):
    pos = np.arange(n_pos)
    tab = np.zeros((n_pos, LANES), np.float32)
    tab[:, 0:AUX_TERMS] = (pos // SEL_BLOCK)[:, None]
    tab[:, AUX_TERMS:2 * AUX_TERMS] = (pos % SEL_BLOCK)[:, None]
    tab[pos, AUX_SEL_ROW + pos // SEL_BLOCK] = 1.0
    return tab


def _aux_query_rows():
    terms = _bf16_terms(LOG2E, AUX_TERMS)
    slopes = _slopes(NSA_HEADS)
    slab_slope = [[slopes[h] for h in range(4)], [slopes[sl // 2] for sl in range(4)],
                  [slopes[2 + sl // 2] for sl in range(4)]]
    out = np.zeros((3, LANES, 4 * QBLOCK), np.float32)
    for g in range(3):
        for sl in range(4):
            for k, c in enumerate(terms):
                out[g, k, sl * QBLOCK:(sl + 1) * QBLOCK] = SEL_BLOCK * slab_slope[g][sl] * c
                out[g, AUX_TERMS + k, sl * QBLOCK:(sl + 1) * QBLOCK] = slab_slope[g][sl] * c
    return out


def _group_mean_matrix():
    idx = np.arange(DIFF_HEADS * DIFF_V) // DIFF_V
    return (idx[:, None] == idx[None, :]).astype(np.float32) / DIFF_V


def _pool_blockdiag(pool_w):
    depth, n, g, _ = pool_w.shape
    out = jnp.zeros((depth, n * g, n * g), pool_w.dtype)
    for k in range(n):
        out = out.at[:, k * g:(k + 1) * g, k * g:(k + 1) * g].set(pool_w[:, k])
    return out.astype(BF)


def kernel(x_prompt, x_sample, cache_kv, state_win_kv, state_pool, state_conv, state_ffn, page_table, c_prompt, c_sample, norm_g, ada_w, ada_b, w_in, cmp_pe, cmp_w1, cmp_w2, diff_lam, diff_norm_g, pool_w, pool_scale, conv_w, w_branch, w_out, w_up, ffn_conv, w_down, final_g):
    depth = w_in.shape[0]
    bp, seq, _ = x_prompt.shape
    bs = x_sample.shape[0]
    n_pages = page_table.shape[1]
    past = n_pages * PAGE_SIZE
    tm = min(ROW_TILE, seq)
    assert seq % tm == 0 and seq % QBLOCK == 0 and seq >= WINDOW + QBLOCK and x_sample.shape[1] == 1

    w_packed, w_gate = _pack_w_in(w_in)
    wc, pe2, w2 = _compress_weights(cmp_pe, cmp_w1, cmp_w2)
    wb = w_branch.astype(BF)
    wo = w_out.astype(BF)
    wu = w_up.astype(BF)
    wd = w_down.astype(BF)
    pw = _pool_blockdiag(pool_w)
    ps = pool_scale.reshape(depth, 1, POOL_W)
    gain = jnp.tile(diff_norm_g, (1, DIFF_HEADS)).reshape(depth, 1, DIFF_HEADS * DIFF_V)
    fg = final_g.reshape(1, D_MODEL)
    gmat = jnp.asarray(_group_mean_matrix())

    n_ch = seq // CMP_STRIDE
    n_sel_p = seq // SEL_BLOCK
    ov_p = np.zeros((n_sel_p, n_ch), np.float32)
    ov_p[:, :n_ch - 1] = _overlap(n_ch - 1, n_sel_p).T
    ov_p = jnp.asarray(ov_p)
    aux_p = jnp.asarray(_aux_key_table(seq), dtype=BF)
    qaux = jnp.asarray(_aux_query_rows())
    n_sel_s = -(-(past + 1) // SEL_BLOCK)
    n_chp = past // CMP_STRIDE
    ov_s = np.zeros((n_chp, LANES), np.float32)
    ov_s[:, :n_sel_s] = _overlap(n_chp, n_sel_s)
    ov_s = jnp.asarray(ov_s)
    emat_s = jnp.asarray(_expand_matrix(LANES, past), dtype=BF)

    mod = _ada_call(jnp.concatenate([c_prompt, c_sample], axis=0), ada_w, ada_b)
    mod_p = mod[:, :bp].reshape(depth, bp, 6, 1, D_MODEL)
    mod_s = jnp.transpose(mod[:, bp:].reshape(depth, bs, 6, D_MODEL), (0, 2, 1, 3)).reshape(depth, 1, 6, bs, D_MODEL)

    x = x_prompt
    kv_p, win_p, pool_p, conv_p, ffn_p = [], [], [], [], []
    for l in range(depth):
        lam_init = 0.8 - 0.6 * math.exp(-0.3 * l)
        g1 = norm_g[l, 0].reshape(1, D_MODEL)
        g2 = norm_g[l, 1].reshape(1, D_MODEL)
        kv32, rest, kvb, kvt = _proj_call(l, x, mod_p, g1, w_packed[l], tm)
        chunks = kvb[:, :, 0:2 * NSA_HD].reshape(bp, n_ch, CMP_STRIDE * 2 * NSA_HD)
        kcvc = _compress_call(chunks, wc[l], pe2[l], w2[l])
        o_a, o_d = _prompt_attn_call(l, rest, kvb, kvt, kcvc, jnp.swapaxes(kcvc, 1, 2),
                                     ov_p, aux_p, qaux, gmat, diff_lam, gain, lam_init)
        x, zin = _merge_call(l, x, mod_p, g1, rest, rest, rest, o_a, o_d, w_gate[l], wb[l], wo[l], pw[l], ps[l],
                             conv_w[l], tm, True)
        x, up_last = _ffn_call(l, x, mod_p, g2, x, wu[l], ffn_conv[l], wd[l], fg, tm, True, l == depth - 1)
        kv_p.append(kv32)
        win_p.append(rest[:, seq - min(WINDOW, seq):, R_WIN:R_WIN + WIN_W])
        pool_p.append(rest[:, seq - POOL_KEEP:, R_UB:R_UB + POOL_W])
        conv_p.append(zin[:, seq - 2:])
        ffn_p.append(up_last[:, SUBLANES - 2:])
    y_prompt = x

    x = x_sample.reshape(1, bs, D_MODEL)
    kv_s, win_s, pool_s, conv_s, ffn_s = [], [], [], [], []
    for l in range(depth):
        lam_init = 0.8 - 0.6 * math.exp(-0.3 * l)
        g1 = norm_g[l, 0].reshape(1, D_MODEL)
        g2 = norm_g[l, 1].reshape(1, D_MODEL)
        kv32, rest, _, _ = _proj_call(l, x, mod_s, g1, w_packed[l], bs)
        o_a, o_d = _sample_attn_call(l, page_table, kv32.reshape(bs, 1, KV_W), rest.reshape(bs, 1, R_W), cache_kv,
                                     state_win_kv, wc[l], pe2[l], w2[l], ov_s, emat_s, gmat, diff_lam, gain,
                                     lam_init, n_sel_s)
        st_conv = jnp.transpose(state_conv[l], (1, 0, 2))
        st_pool = jnp.transpose(state_pool[l], (1, 0, 2))
        st_ffn = jnp.transpose(state_ffn[l], (1, 0, 2))
        x, zin = _merge_call(l, x, mod_s, g1, rest, st_conv, st_pool, o_a.reshape(1, bs, BRANCH_W),
                             o_d.reshape(1, bs, BRANCH_W), w_gate[l], wb[l], wo[l], pw[l], ps[l], conv_w[l], bs, False)
        x, up_new = _ffn_call(l, x, mod_s, g2, st_ffn, wu[l], ffn_conv[l], wd[l], fg, bs, False, l == depth - 1)
        kv_s.append(kv32.reshape(bs, 1, KV_W))
        win_s.append(rest[0, :, R_WIN:R_WIN + WIN_W].reshape(bs, 1, WIN_W))
        pool_s.append(rest[0, :, R_UB:R_UB + POOL_W].reshape(bs, 1, POOL_W))
        conv_s.append(zin.reshape(bs, 1, CONV_W))
        ffn_s.append(up_new.reshape(bs, 1, 2 * D_FF))
    y_sample = x.reshape(bs, 1, D_MODEL)

    def roll_state(state, new_rows, keep):
        tail = state[:, :, state.shape[2] - keep:]
        return jnp.roll(tail, -1, axis=2).at[:, :, keep - 1:].set(jnp.stack(new_rows))

    return (y_prompt, y_sample, jnp.stack(kv_p), jnp.stack(win_p), jnp.stack(pool_p), jnp.stack(conv_p),
            jnp.stack(ffn_p), jnp.stack(kv_s),
            roll_state(state_win_kv, win_s, min(WINDOW, past + 1)), roll_state(state_pool, pool_s, POOL_KEEP),
            roll_state(state_conv, conv_s, 2), roll_state(state_ffn, ffn_s, 2))
```

```python
import functools
import math

import numpy as np
import jax
import jax.numpy as jnp
from jax import lax
from jax.experimental import pallas as pl
from jax.experimental.pallas import tpu as pltpu

D_MODEL = 1024
NSA_HEADS = 4
NSA_HD = 64
CMP_STRIDE = 16
CMP_BLOCK = 32
CMP_HID = 128
SEL_BLOCK = 64
SEL_TOP_N = 16
WINDOW = 512
POOL_WINDOWS = (2, 4, 8, 16)
POOL_GROUP = 64
POOL_W = 256
POOL_KEEP = 15
CONV_W = 256
DIFF_HEADS = 4
DIFF_QK = 32
DIFF_V = 64
N_BRANCH = 4
BRANCH_W = 256
D_FF = 2816
QBLOCK = 128
PAGE_SIZE = 128
EPS = 1e-6
KV_W = 768
WIN_W = 128
GA_W = 12

P_KV, P_CV, P_QA, P_UB, P_QD, P_WIN, P_GA, P_W = 0, 768, 1536, 1792, 2048, 2304, 2432, 2560
KVB_W = KV_W + WIN_W
R_CV, R_QA, R_UB, R_QD, R_WIN, R_GA, R_W = [p - KV_W for p in (P_CV, P_QA, P_UB, P_QD, P_WIN, P_GA, P_W)]

LANES = 128
SUBLANES = 8
VMEM_LIMIT = 56 * 1024 * 1024

NEG = -1e30
BIG = 1e30
LOG2E = math.log2(math.e)
BF = jnp.bfloat16
F32 = jnp.float32
HI = lax.Precision.HIGHEST

ROW_TILE = 512
KEY_CHUNK = 256
FF_CHUNK = 2816
SAMPLE_GROUP = 2
AUX_TERMS = 4
AUX_SEL_ROW = 64
HALO = 16
FFN_HALO = 8

_NT = (((1,), (1,)), ((), ()))


def _dot(a, b, precision=None):
    return jnp.dot(a, b, preferred_element_type=F32, precision=precision)


def _dot_nt(a, b, precision=None):
    return lax.dot_general(a, b, _NT, preferred_element_type=F32, precision=precision)


def _slopes(n):
    return [float(2.0 ** (-8.0 * (k + 1) / n)) for k in range(n)]


def _norm_mod(x, g, scale, shift):
    ms = jnp.mean(x * x, axis=-1, keepdims=True)
    return (x * lax.rsqrt(ms + EPS) * g) * (1.0 + scale) + shift


def _cparams(sem):
    return pltpu.CompilerParams(dimension_semantics=sem, vmem_limit_bytes=VMEM_LIMIT)


def _const_spec(shape):
    nd = len(shape)
    return pl.BlockSpec(shape, lambda *a: (0,) * nd, pipeline_mode=pl.Buffered(1))


def _ada_kernel(c_ref, w_ref, b_ref, o_ref):
    c = c_ref[...]
    a = (c * jax.nn.sigmoid(c)).astype(BF)
    o_ref[0] = _dot(a, w_ref[0].astype(BF)) + b_ref[0]


def _ada_call(c_all, ada_w, ada_b):
    depth = ada_w.shape[0]
    nb = c_all.shape[0]
    return pl.pallas_call(
        _ada_kernel,
        out_shape=jax.ShapeDtypeStruct((depth, nb, 6 * D_MODEL), F32),
        grid=(depth, 6),
        in_specs=[
            pl.BlockSpec((nb, D_MODEL), lambda l, j: (0, 0)),
            pl.BlockSpec((1, D_MODEL, D_MODEL), lambda l, j: (l, 0, j)),
            pl.BlockSpec((1, 1, D_MODEL), lambda l, j: (l, 0, j)),
        ],
        out_specs=pl.BlockSpec((1, nb, D_MODEL), lambda l, j: (l, 0, j)),
        compiler_params=_cparams(("arbitrary", "arbitrary")),
    )(c_all, ada_w, ada_b.reshape(depth, 1, 6 * D_MODEL))


def _proj_kernel(x_ref, sc_ref, sh_ref, g_ref, w_ref, kv_ref, rest_ref, kvb_ref, kvt_ref):
    h = _norm_mod(x_ref[0], g_ref[...], sc_ref[0, 0, 0], sh_ref[0, 0, 0]).astype(BF)
    y = _dot(h, w_ref[...])
    kv_ref[0] = y[:, P_KV:P_KV + KV_W]
    rest_ref[0] = y[:, KV_W:]
    att = jnp.concatenate([y[:, P_KV:P_KV + KV_W], y[:, P_WIN:P_WIN + WIN_W]], axis=1)
    kvb_ref[0] = att.astype(BF)
    kvt_ref[0] = att.T.astype(BF)


def _mod_spec(l, which, rm):
    return pl.BlockSpec((1, 1, 1, rm, D_MODEL), lambda b, i: (l, b, which, 0, 0))


def _proj_call(l, x3, mod, g, w_packed, tm):
    bx, tx, _ = x3.shape
    rm = mod.shape[3]
    return pl.pallas_call(
        _proj_kernel,
        out_shape=(jax.ShapeDtypeStruct((bx, tx, KV_W), F32), jax.ShapeDtypeStruct((bx, tx, R_W), F32),
                   jax.ShapeDtypeStruct((bx, tx, KVB_W), BF), jax.ShapeDtypeStruct((bx, KVB_W, tx), BF)),
        grid=(bx, tx // tm),
        in_specs=[
            pl.BlockSpec((1, tm, D_MODEL), lambda b, i: (b, i, 0)),
            _mod_spec(l, 1, rm), _mod_spec(l, 0, rm),
            _const_spec((1, D_MODEL)),
            _const_spec((D_MODEL, P_W)),
        ],
        out_specs=(pl.BlockSpec((1, tm, KV_W), lambda b, i: (b, i, 0)),
                   pl.BlockSpec((1, tm, R_W), lambda b, i: (b, i, 0)),
                   pl.BlockSpec((1, tm, KVB_W), lambda b, i: (b, i, 0)),
                   pl.BlockSpec((1, KVB_W, tm), lambda b, i: (b, 0, i))),
        compiler_params=_cparams(("arbitrary", "arbitrary")),
    )(x3, mod, mod, g, w_packed)


def _compress_finish(y, yb, w2):
    n = y.shape[0]
    yn = pltpu.roll(y, n - 1, axis=0)
    hid = CMP_HID
    hk = y[:, 0:hid] + yn[:, hid:2 * hid] + (yb[0:1, 0:hid] + yb[1:2, hid:2 * hid])
    hv = y[:, 2 * hid:3 * hid] + yn[:, 3 * hid:4 * hid] + (yb[0:1, 2 * hid:3 * hid] + yb[1:2, 3 * hid:4 * hid])
    act = jnp.concatenate([jax.nn.gelu(hk), jax.nn.gelu(hv)], axis=1).astype(BF)
    return _dot(act, w2)


def _compress_kernel(ch_ref, wc_ref, pe_ref, w2_ref, o_ref):
    y = _dot(ch_ref[0], wc_ref[...])
    yb = _dot(pe_ref[...], wc_ref[...])
    o_ref[0] = _compress_finish(y, yb, w2_ref[...]).astype(BF)


def _compress_call(chunks, wc, pe2, w2):
    b, n_ch, cw = chunks.shape
    return pl.pallas_call(
        _compress_kernel,
        out_shape=jax.ShapeDtypeStruct((b, n_ch, 2 * NSA_HD), BF),
        grid=(b,),
        in_specs=[pl.BlockSpec((1, n_ch, cw), lambda i: (i, 0, 0)),
                  _const_spec(wc.shape), _const_spec(pe2.shape), _const_spec(w2.shape)],
        out_specs=pl.BlockSpec((1, n_ch, 2 * NSA_HD), lambda i: (i, 0, 0)),
        compiler_params=_cparams(("arbitrary",)),
    )(chunks, wc, pe2, w2)


def _lam_value(lam_ref, lam_init):
    lp = lam_ref[0]
    a = jnp.sum(lp[0:1] * lp[1:2], axis=-1, keepdims=True)
    b = jnp.sum(lp[2:3] * lp[3:4], axis=-1, keepdims=True)
    return jnp.exp(a) - jnp.exp(b) + lam_init


def _group_rms(od, gmat, gain, lam_init):
    ms = _dot(od * od, gmat, precision=HI)
    return od * lax.rsqrt(ms + EPS) * gain * (1.0 - lam_init)


def _prompt_attn_kernel(qa_ref, qd_ref, ga_ref, kvb_ref, kvt_ref, kc_ref, kct_ref, ov_ref, aux_ref, qaux_ref,
                        gmat_ref, lam_ref, gain_ref, oa_ref, od_ref, m_sc, l_sc, acc_sc, s_sc, p_sc, a_sc, pre_sc, *,
                        lam_init, top):
    tq = QBLOCK
    kc_len = KEY_CHUNK
    i = pl.program_id(1)
    t0 = i * tq
    n_sel = ov_ref.shape[0]

    t_row = t0 + lax.broadcasted_iota(jnp.int32, (1, LANES), 1)
    feat = lax.broadcasted_iota(jnp.int32, (LANES, LANES), 0)

    qa = qa_ref[0] * (NSA_HD ** -0.5 * LOG2E)
    qa_t = [qa[:, 0:LANES].T, qa[:, LANES:2 * LANES].T]
    slabs = []
    for h in range(NSA_HEADS):
        tile = qa_t[h // 2]
        if h % 2:
            tile = pltpu.roll(tile, NSA_HD, axis=0)
        slabs.append(jnp.where(feat < NSA_HD, tile, 0.0))
    q_nsa = jnp.concatenate(slabs, axis=1).astype(BF)

    qd = qd_ref[0] * (DIFF_QK ** -0.5 * LOG2E)
    q_diff = []
    for g in range(2):
        tile = qd[:, g * LANES:(g + 1) * LANES].T
        slabs = [jnp.where(lax.shift_right_logical(feat, 5) == sl, tile, 0.0) for sl in range(4)]
        q_diff.append(jnp.concatenate(slabs, axis=1).astype(BF))

    def lanes_of(x, sl):
        return x[:, sl * tq:(sl + 1) * tq]

    kc = kc_ref[0]
    n_ch = kc.shape[0]
    s = _dot(kc, q_nsa)
    cmp_end = lax.broadcasted_iota(jnp.int32, (n_ch, LANES), 0) * CMP_STRIDE + (CMP_BLOCK - 1)
    vis = cmp_end <= t_row
    p_heads = []
    for h in range(NSA_HEADS):
        sh = jnp.where(vis, lanes_of(s, h), NEG)
        m = jnp.max(sh, axis=0, keepdims=True)
        e = jnp.where(vis, jnp.exp2(sh - m), 0.0)
        d = jnp.sum(e, axis=0, keepdims=True)
        p_heads.append(e / jnp.where(d > 0, d, 1.0))
    o_cmp = _dot(kct_ref[0], jnp.concatenate(p_heads, axis=1).astype(BF))
    p_sum = p_heads[0] + p_heads[1] + p_heads[2] + p_heads[3]
    imp_t = _dot(ov_ref[...], p_sum, precision=HI)

    j_full = lax.broadcasted_iota(jnp.int32, (n_sel, LANES), 0)
    valid = j_full * SEL_BLOCK <= t_row
    cur = lax.shift_right_logical(t_row, 6)
    forced = jnp.where(j_full == 0, 1, jnp.where(j_full == cur, 1, jnp.where(j_full == cur - 1, 1, 0)))
    score = jnp.where(valid, jnp.where(forced > 0, BIG, imp_t), -BIG)
    groups = [score[SUBLANES * v:SUBLANES * (v + 1)] for v in range(n_sel // SUBLANES)]
    counts = [jnp.zeros((SUBLANES, LANES), F32) for _ in groups]
    j_loc = lax.broadcasted_iota(jnp.int32, (SUBLANES, LANES), 0)
    for ii in range(n_sel):
        si = score[ii:ii + 1, :]
        for v, sv in enumerate(groups):
            ge = jnp.where(si >= sv, 1.0, 0.0)
            gt = jnp.where(si > sv, 1.0, 0.0)
            if v > ii // SUBLANES:
                inc = ge
            elif v < ii // SUBLANES:
                inc = gt
            else:
                inc = jnp.where(j_loc > ii % SUBLANES, ge, gt)
            counts[v] = counts[v] + inc
    rank = jnp.concatenate(counts, axis=0)
    sel_neg = jnp.where(rank < top, 0.0, NEG)

    m_sc[...] = jnp.full(m_sc.shape, NEG, F32)
    l_sc[...] = jnp.zeros(l_sc.shape, F32)
    acc_sc[...] = jnp.zeros(acc_sc.shape, F32)

    aux_nsa = qaux_ref[0]
    pad = [jnp.zeros((AUX_SEL_ROW - n_sel, 4 * tq), F32)] if n_sel < AUX_SEL_ROW else []
    aux_sel = jnp.concatenate([aux_nsa[0:AUX_SEL_ROW], jnp.concatenate([sel_neg] * 4, axis=1)] + pad, axis=0)
    q_win = jnp.concatenate([q_nsa, aux_nsa.astype(BF)], axis=0)
    k_tile = [1, 2, 3]
    v_tile = [1, 4, 5]
    q_grp = [jnp.concatenate([q_nsa, aux_sel.astype(BF)], axis=0),
             jnp.concatenate([q_diff[0], qaux_ref[1].astype(BF)], axis=0),
             jnp.concatenate([q_diff[1], qaux_ref[2].astype(BF)], axis=0)]
    n_full = t0 // kc_len

    def scores(c, buf):
        ks = pl.multiple_of(c * kc_len, kc_len)
        aux_k = aux_ref[pl.ds(ks, kc_len), :]
        for grp in range(3):
            k_op = jnp.concatenate(
                [kvb_ref[0, pl.ds(ks, kc_len), k_tile[grp] * LANES:(k_tile[grp] + 1) * LANES], aux_k], axis=1)
            s_sc[buf, grp] = _dot(k_op, q_grp[grp])

    def softmax(c, buf, diagonal):
        if diagonal:
            pos = c * kc_len + lax.broadcasted_iota(jnp.int32, (kc_len, LANES), 0)
            causal_bias = jnp.where(pos <= t_row, 0.0, NEG)
        for grp in range(3):
            m_old_all = m_sc[grp]
            l_old_all = l_sc[grp]
            alphas, ms, ls = [], [], []
            for sl in range(4):
                s_h = s_sc[buf, grp, :, sl * tq:(sl + 1) * tq]
                if diagonal:
                    s_h = s_h + causal_bias
                m_old = lanes_of(m_old_all, sl)
                m_new = jnp.maximum(m_old, jnp.max(s_h, axis=0, keepdims=True))
                alpha = jnp.exp2(m_old - m_new)
                e = jnp.exp2(s_h - m_new)
                ls.append(alpha * lanes_of(l_old_all, sl) + jnp.sum(e, axis=0, keepdims=True))
                ms.append(m_new)
                alphas.append(alpha)
                p_sc[buf, grp, :, sl * tq:(sl + 1) * tq] = e.astype(BF)
            m_sc[grp] = jnp.concatenate(ms, axis=1)
            l_sc[grp] = jnp.concatenate(ls, axis=1)
            a_sc[buf, grp] = jnp.concatenate(alphas, axis=1)

    def values(c, buf):
        ks = pl.multiple_of(jnp.maximum(c, 0) * kc_len, kc_len)
        for grp in range(3):
            v_t = kvt_ref[0, v_tile[grp] * LANES:(v_tile[grp] + 1) * LANES, pl.ds(ks, kc_len)]
            acc_sc[grp] = a_sc[buf, grp] * acc_sc[grp] + _dot(v_t, p_sc[buf, grp])

    def stage(c, buf):
        scores(c + 1, 1 - buf)
        values(c - 1, 1 - buf)
        softmax(c, buf, False)

    def last(c, buf):
        values(c - 1, 1 - buf)
        softmax(c, buf, True)
        values(c, buf)

    span = WINDOW + tq
    start = pl.multiple_of(jnp.maximum(i - WINDOW // tq, 0) * tq, tq)
    pos_w = start + lax.broadcasted_iota(jnp.int32, (span, LANES), 0)
    dist = t_row - pos_w
    mb_w = jnp.where(dist >= 0, jnp.where(dist < WINDOW, 0.0, NEG), NEG)
    k_win = jnp.concatenate([kvb_ref[0, pl.ds(start, span), KV_W:KV_W + WIN_W], aux_ref[pl.ds(start, span), :]],
                            axis=1)
    sw = _dot(k_win, q_win)
    pw, inv_w = [], []
    for h in range(NSA_HEADS):
        s_h = lanes_of(sw, h) + mb_w
        e = jnp.exp2(s_h - jnp.max(s_h, axis=0, keepdims=True))
        inv_w.append(1.0 / jnp.sum(e, axis=0, keepdims=True))
        pw.append(e.astype(BF))
    o_win = (_dot(kvt_ref[0, KV_W:KV_W + WIN_W, pl.ds(start, span)], jnp.concatenate(pw, axis=1))
             * jnp.concatenate(inv_w, axis=1))
    ga_t = jax.nn.sigmoid(ga_ref[0]).T
    pre_sc[...] = jnp.concatenate(
        [ga_t[3 * h:3 * h + 1] * lanes_of(o_cmp, h) + ga_t[3 * h + 2:3 * h + 3] * lanes_of(o_win, h)
         for h in range(NSA_HEADS)], axis=1)

    scores(0, 0)
    p_sc[1] = jnp.zeros(p_sc.shape[1:], BF)
    a_sc[1] = jnp.ones(a_sc.shape[1:], F32)
    n_pairs = n_full // 2

    def pair(k, carry):
        stage(2 * k, 0)
        stage(2 * k + 1, 1)
        return carry

    lax.fori_loop(0, n_pairs, pair, 0)

    @pl.when(n_full == 2 * n_pairs)
    def _():
        last(n_full, 0)

    @pl.when(n_full != 2 * n_pairs)
    def _():
        stage(n_full - 1, 0)
        last(n_full, 1)

    o_sel = acc_sc[0] / l_sc[0]

    pre = pre_sc[...]
    oa_h = [lanes_of(pre, h) + ga_t[3 * h + 1:3 * h + 2] * lanes_of(o_sel, h) for h in range(NSA_HEADS)]
    oa_ref[0] = jnp.concatenate(
        [jnp.concatenate([oa_h[0][NSA_HD:], oa_h[1][NSA_HD:]], axis=0).T,
         jnp.concatenate([oa_h[2][NSA_HD:], oa_h[3][NSA_HD:]], axis=0).T], axis=1)

    lam = _lam_value(lam_ref, lam_init)
    tiles = []
    for g in range(2):
        on = acc_sc[1 + g] / l_sc[1 + g]
        a0 = lanes_of(on, 0) - lam * lanes_of(on, 1)
        a1 = lanes_of(on, 2) - lam * lanes_of(on, 3)
        tiles.append(jnp.concatenate([a0[:DIFF_V], a1[DIFF_V:]], axis=0).T)
    od = jnp.concatenate(tiles, axis=1)
    od_ref[0] = _group_rms(od, gmat_ref[...], gain_ref[0], lam_init)


def _prompt_attn_call(l, proj, kvb, kvt, kcvc, kct, ov, aux, qaux, gmat, diff_lam, gain, lam_init):
    b, t, _ = proj.shape
    n_ch = kcvc.shape[1]
    n_sel = ov.shape[0]
    tq = QBLOCK
    assert n_sel <= LANES - AUX_SEL_ROW and KEY_CHUNK % tq == 0
    kern = functools.partial(_prompt_attn_kernel, lam_init=lam_init, top=min(SEL_TOP_N, n_sel))
    return pl.pallas_call(
        kern,
        out_shape=(jax.ShapeDtypeStruct((b, t, BRANCH_W), F32), jax.ShapeDtypeStruct((b, t, BRANCH_W), F32)),
        grid=(b, t // tq),
        in_specs=[
            pl.BlockSpec((1, tq, 256), lambda bi, i: (bi, i, R_QA // 256)),
            pl.BlockSpec((1, tq, 256), lambda bi, i: (bi, i, R_QD // 256)),
            pl.BlockSpec((1, tq, LANES), lambda bi, i: (bi, i, R_GA // LANES)),
            pl.BlockSpec((1, t, KVB_W), lambda bi, i: (bi, 0, 0)),
            pl.BlockSpec((1, KVB_W, t), lambda bi, i: (bi, 0, 0)),
            pl.BlockSpec((1, n_ch, 2 * NSA_HD), lambda bi, i: (bi, 0, 0)),
            pl.BlockSpec((1, 2 * NSA_HD, n_ch), lambda bi, i: (bi, 0, 0)),
            _const_spec(ov.shape), _const_spec(aux.shape), _const_spec(qaux.shape), _const_spec(gmat.shape),
            pl.BlockSpec((1, 4, DIFF_QK), lambda bi, i: (l, 0, 0)),
            pl.BlockSpec((1, 1, BRANCH_W), lambda bi, i: (l, 0, 0)),
        ],
        out_specs=(pl.BlockSpec((1, tq, BRANCH_W), lambda bi, i: (bi, i, 0)),
                   pl.BlockSpec((1, tq, BRANCH_W), lambda bi, i: (bi, i, 0))),
        scratch_shapes=[pltpu.VMEM((3, 1, 4 * tq), F32), pltpu.VMEM((3, 1, 4 * tq), F32),
                        pltpu.VMEM((3, LANES, 4 * tq), F32),
                        pltpu.VMEM((2, 3, KEY_CHUNK, 4 * tq), F32), pltpu.VMEM((2, 3, KEY_CHUNK, 4 * tq), BF),
                        pltpu.VMEM((2, 3, 1, 4 * tq), F32), pltpu.VMEM((LANES, 4 * tq), F32)],
        compiler_params=_cparams(("arbitrary", "arbitrary")),
    )(proj, proj, proj, kvb, kvt, kcvc, kct, ov, aux, qaux, gmat, diff_lam, gain)


def _sample_attn_seq(kvn_ref, rest_ref, *refs, n_pages, lam_init, top, n_sel):
    pages = refs[:n_pages]
    (swin_ref, wc_ref, pe_ref, w2_ref, ov_ref, e_ref, gmat_ref, lam_ref, gain_ref, oa_ref, od_ref,
     cmp_rows) = refs[n_pages:]
    past = n_pages * PAGE_SIZE
    n_chp = past // CMP_STRIDE
    n_vis = (past - (CMP_BLOCK - 1)) // CMP_STRIDE + 1
    cur = past // SEL_BLOCK
    nsa_slopes = _slopes(NSA_HEADS)
    diff_slopes = _slopes(DIFF_HEADS)

    row = jnp.concatenate([kvn_ref[0], rest_ref[0]], axis=1)
    kv_new = row[:, P_KV:P_KV + KV_W]
    lane = lax.broadcasted_iota(jnp.int32, (1, LANES), 1)
    lo = lane < NSA_HD
    rid = lax.broadcasted_iota(jnp.int32, (SUBLANES, 1), 0)
    rid_full = lax.broadcasted_iota(jnp.int32, (SUBLANES, LANES), 0)

    def bcast(v):
        return jnp.broadcast_to(v, (SUBLANES, v.shape[1]))

    qa = row[:, P_QA:P_QA + 256] * (NSA_HD ** -0.5)
    b0, b1 = bcast(qa[:, 0:LANES]), bcast(qa[:, LANES:2 * LANES])
    cand = [b0, pltpu.roll(b0, NSA_HD, axis=1), b1, pltpu.roll(b1, NSA_HD, axis=1)]
    q8 = jnp.zeros((SUBLANES, LANES), F32)
    for h in range(NSA_HEADS):
        q8 = jnp.where(rid_full == h, cand[h], q8)
    q8 = jnp.where(lo, q8, 0.0)
    q8b = q8.astype(BF)
    slope_col = jnp.zeros((SUBLANES, 1), F32)
    for h in range(NSA_HEADS):
        slope_col = jnp.where(rid == h, nsa_slopes[h], slope_col)

    for j, pg in enumerate(pages):
        cmp_rows[j * PAGE_SIZE:(j + 1) * PAGE_SIZE, :] = pg[0, 0, :, 0:LANES]
    chunks = jnp.concatenate([cmp_rows[pl.ds(r, n_chp, stride=CMP_STRIDE), :].astype(BF)
                              for r in range(CMP_STRIDE)], axis=1)
    y = yield chunks
    yb = _dot(pe_ref[...], wc_ref[...])
    kc = _compress_finish(y, yb, w2_ref[...]).astype(BF)
    yield

    cl = lax.broadcasted_iota(jnp.int32, (1, n_chp), 1)
    vis = cl < n_vis
    s = jnp.where(vis, _dot_nt(q8b, kc), NEG)
    m = jnp.max(s, axis=-1, keepdims=True)
    e = jnp.where(vis, jnp.exp(s - m), 0.0)
    d = jnp.sum(e, axis=-1, keepdims=True)
    p_c = jnp.where(rid < NSA_HEADS, e / jnp.where(d > 0, d, 1.0), 0.0)
    o_cmp = _dot(p_c.astype(BF), kc)
    imp = jnp.sum(_dot(p_c, ov_ref[...], precision=HI), axis=0, keepdims=True)
    yield

    forced = jnp.where(lane == 0, 1, jnp.where(lane == cur, 1, jnp.where(lane == cur - 1, 1, 0)))
    score = jnp.where(lane < n_sel, jnp.where(forced > 0, BIG, imp), -BIG)
    r_row = jnp.broadcast_to(score, (LANES, LANES))
    r_col = r_row.T
    ii = lax.broadcasted_iota(jnp.int32, (LANES, LANES), 0)
    jj = lax.broadcasted_iota(jnp.int32, (LANES, LANES), 1)
    inc = jnp.where(r_col > r_row, 1.0, jnp.where(r_col == r_row, jnp.where(ii < jj, 1.0, 0.0), 0.0))
    rank = jnp.sum(inc, axis=0, keepdims=True)
    sel = jnp.where(rank < top, 1.0, 0.0)
    sel_exp = _dot(bcast(sel).astype(BF), e_ref[...])
    yield

    pos = lax.broadcasted_iota(jnp.int32, (1, past), 1)
    rel = (pos - past).astype(F32)

    def attend_new(q, sc, k_tile, kv_tile_new):
        s_new = jnp.sum(q * kv_tile_new, axis=-1, keepdims=True)
        mm = jnp.maximum(jnp.max(sc, axis=-1, keepdims=True), s_new)
        ee = jnp.exp(sc - mm)
        e_new = jnp.exp(s_new - mm)
        den = jnp.sum(ee, axis=-1, keepdims=True) + e_new
        return (_dot(ee.astype(BF), k_tile) + e_new * kv_tile_new) / den

    k_sel = jnp.concatenate([pg[0, 0, :, LANES:2 * LANES] for pg in pages], axis=0).astype(BF)
    sc = _dot_nt(q8b, k_sel) + slope_col * rel + jnp.where(sel_exp > 0.5, 0.0, NEG)
    o_sel = attend_new(q8, sc, k_sel, kv_new[:, LANES:2 * LANES])
    yield

    k_win = swin_ref[0, 0].astype(BF)
    nw = k_win.shape[0]
    wl = lax.broadcasted_iota(jnp.int32, (1, nw), 1)
    dist_w = nw - wl
    sc = _dot_nt(q8b, k_win) - slope_col * dist_w.astype(F32) + jnp.where(dist_w < WINDOW, 0.0, NEG)
    o_win = attend_new(q8, sc, k_win, row[:, P_WIN:P_WIN + WIN_W])

    ga8 = bcast(jax.nn.sigmoid(row[:, P_GA:P_GA + LANES]))
    gates = [jnp.sum(jnp.where(lane == 3 * rid_full + k, ga8, 0.0), axis=-1, keepdims=True) for k in range(3)]
    oa8 = gates[0] * o_cmp + gates[1] * o_sel + gates[2] * o_win
    oa_ref[0] = jnp.concatenate(
        [jnp.where(lo, pltpu.roll(oa8[0:1], NSA_HD, axis=1), oa8[1:2]),
         jnp.where(lo, pltpu.roll(oa8[2:3], NSA_HD, axis=1), oa8[3:4])], axis=1)
    yield

    qd = bcast(row[:, P_QD:P_QD + 256] * (DIFF_QK ** -0.5))
    lane2 = lax.broadcasted_iota(jnp.int32, (SUBLANES, 2 * LANES), 1)
    rid2 = lax.broadcasted_iota(jnp.int32, (SUBLANES, 2 * LANES), 0)
    qd8 = jnp.where(lax.shift_right_logical(lane2, 5) == rid2, qd, 0.0)
    dslope = jnp.zeros((SUBLANES, 1), F32)
    for r in range(SUBLANES):
        dslope = jnp.where(rid == r, diff_slopes[r // 2], dslope)
    k_d = jnp.concatenate([pg[0, 0, :, 2 * LANES:4 * LANES] for pg in pages], axis=0).astype(BF)
    v_d = jnp.concatenate([pg[0, 0, :, 4 * LANES:6 * LANES] for pg in pages], axis=0).astype(BF)
    sc = _dot_nt(qd8.astype(BF), k_d) + dslope * rel
    s_new = jnp.sum(qd8 * kv_new[:, 2 * LANES:4 * LANES], axis=-1, keepdims=True)
    mm = jnp.maximum(jnp.max(sc, axis=-1, keepdims=True), s_new)
    ee = jnp.exp(sc - mm)
    e_new = jnp.exp(s_new - mm)
    den = jnp.sum(ee, axis=-1, keepdims=True) + e_new
    o8 = (_dot(ee.astype(BF), v_d) + e_new * kv_new[:, 4 * LANES:6 * LANES]) / den
    lam = _lam_value(lam_ref, lam_init)
    coef = jnp.where((rid2 & 1) == 0, 1.0, -lam)
    head_mask = lax.shift_right_logical(lane2, 6) == lax.shift_right_logical(rid2, 1)
    od = jnp.sum(jnp.where(head_mask, coef * o8, 0.0), axis=0, keepdims=True)
    od8 = _group_rms(bcast(od), gmat_ref[...], gain_ref[0], lam_init)
    od_ref[0] = od8[0:1]


def _sample_group_kernel(pt_ref, kvn_ref, rest_ref, *refs, n_pages, group, **kw):
    pages = refs[:group * n_pages]
    swin_ref, *consts, oa_ref, od_ref, cmp_rows = refs[group * n_pages:]
    seqs = []
    for s in range(group):
        one = pl.ds(s, 1)
        seqs.append(_sample_attn_seq(kvn_ref.at[one], rest_ref.at[one],
                                     *pages[s * n_pages:(s + 1) * n_pages], swin_ref.at[:, one],
                                     *consts, oa_ref.at[one], od_ref.at[one], cmp_rows.at[s], n_pages=n_pages, **kw))
    chunks = [next(seq) for seq in seqs]
    n_chp = chunks[0].shape[0]
    y_all = _dot(jnp.concatenate(chunks, axis=0), consts[0][...])
    for s, seq in enumerate(seqs):
        seq.send(y_all[s * n_chp:(s + 1) * n_chp])
    live = True
    while live:
        live = False
        for seq in seqs:
            live = next(seq, "done") != "done" or live


def _sample_attn_call(l, page_table, kv_new, rest, cache_kv, state_win, wc, pe2, w2, ov, emat, gmat, diff_lam, gain,
                      lam_init, n_sel):
    bs = rest.shape[0]
    n_pages = page_table.shape[1]
    nw = state_win.shape[2]
    group = SAMPLE_GROUP if bs % SAMPLE_GROUP == 0 else 1
    kern = functools.partial(_sample_group_kernel, n_pages=n_pages, group=group, lam_init=lam_init,
                             top=min(SEL_TOP_N, n_sel), n_sel=n_sel)

    def page_spec(s, j):
        return pl.BlockSpec((1, 1, PAGE_SIZE, KV_W), lambda b, pt: (l, pt[group * b + s, j], 0, 0))

    def cspec(shape):
        nd = len(shape)
        return pl.BlockSpec(shape, lambda b, pt: (0,) * nd, pipeline_mode=pl.Buffered(1))

    grid_spec = pltpu.PrefetchScalarGridSpec(
        num_scalar_prefetch=1,
        grid=(bs // group,),
        in_specs=[pl.BlockSpec((group, 1, KV_W), lambda b, pt: (b, 0, 0)),
                  pl.BlockSpec((group, 1, R_W), lambda b, pt: (b, 0, 0))]
        + [page_spec(s, j) for s in range(group) for j in range(n_pages)]
        + [pl.BlockSpec((1, group, nw, WIN_W), lambda b, pt: (l, b, 0, 0)),
           cspec(wc.shape), cspec(pe2.shape), cspec(w2.shape), cspec(ov.shape), cspec(emat.shape),
           cspec(gmat.shape),
           pl.BlockSpec((1, 4, DIFF_QK), lambda b, pt: (l, 0, 0)),
           pl.BlockSpec((1, 1, BRANCH_W), lambda b, pt: (l, 0, 0))],
        out_specs=(pl.BlockSpec((group, 1, BRANCH_W), lambda b, pt: (b, 0, 0)),
                   pl.BlockSpec((group, 1, BRANCH_W), lambda b, pt: (b, 0, 0))),
        scratch_shapes=[pltpu.VMEM((group, n_pages * PAGE_SIZE, LANES), F32)],
    )
    return pl.pallas_call(
        kern,
        out_shape=(jax.ShapeDtypeStruct((bs, 1, BRANCH_W), F32), jax.ShapeDtypeStruct((bs, 1, BRANCH_W), F32)),
        grid_spec=grid_spec,
        compiler_params=_cparams(("arbitrary",)),
    )(page_table, kv_new, rest, *([cache_kv] * (group * n_pages)), state_win, wc, pe2, w2, ov, emat, gmat, diff_lam, gain)


def _merge_kernel(*refs, halo):
    if halo:
        (x_ref, sc_ref, sh_ref, gt_ref, g_ref, cv_ref, ub_ref, cvp_ref, ubp_ref, oa_ref, od_ref,
         wg_ref, wb_ref, wo_ref, pw_ref, ps_ref, cw_ref, xo_ref, zin_ref) = refs
    else:
        (x_ref, sc_ref, sh_ref, gt_ref, g_ref, cv_ref, ub_ref, stc_ref, stp_ref, oa_ref, od_ref,
         wg_ref, wb_ref, wo_ref, pw_ref, ps_ref, cw_ref, xo_ref, zin_ref) = refs
    i = pl.program_id(1)
    x = x_ref[0]
    tm = x.shape[0]
    h = _norm_mod(x, g_ref[...], sc_ref[0, 0, 0], sh_ref[0, 0, 0]).astype(BF)

    cv = cv_ref[0]
    u = ub_ref[0]
    bg = cv[:, 0:CONV_W]
    zin = cv[:, CONV_W:2 * CONV_W] * cv[:, 2 * CONV_W:3 * CONV_W]
    zin_ref[0] = zin
    cw = cw_ref[...]
    lane = lax.broadcasted_iota(jnp.int32, (1, POOL_W), 1)
    win_lane = jnp.where(lane < 64, 2, jnp.where(lane < 128, 4, jnp.where(lane < 192, 8, 16)))

    if halo:
        keep = jnp.where(i > 0, 1.0, 0.0)
        cvp = cvp_ref[0] * keep
        ubp = ubp_ref[0] * keep
        z_ext = jnp.concatenate([cvp[:, CONV_W:2 * CONV_W] * cvp[:, 2 * CONV_W:3 * CONV_W], zin], axis=0)
        z = (cw[0:1] * pltpu.roll(z_ext, 2, axis=0) + cw[1:2] * pltpu.roll(z_ext, 1, axis=0)
             + cw[2:3] * z_ext)[HALO:]
        ext = jnp.concatenate([ubp, u], axis=0)
        s2 = ext + pltpu.roll(ext, 1, axis=0)
        s4 = s2 + pltpu.roll(s2, 2, axis=0)
        s8 = s4 + pltpu.roll(s4, 4, axis=0)
        s16 = s8 + pltpu.roll(s8, 8, axis=0)
        tot = jnp.where(lane < 64, s2, jnp.where(lane < 128, s4, jnp.where(lane < 192, s8, s16)))[HALO:]
        pos1 = i * tm + lax.broadcasted_iota(jnp.int32, (tm, 1), 0) + 1
        cnt = jnp.minimum(win_lane, pos1).astype(F32)
    else:
        z = cw[0:1] * stc_ref[0] + cw[1:2] * stc_ref[1] + cw[2:3] * zin
        s2 = u + stp_ref[POOL_KEEP - 1]
        s4 = s2 + stp_ref[POOL_KEEP - 2] + stp_ref[POOL_KEEP - 3]
        s8 = s4
        for k in range(4, 8):
            s8 = s8 + stp_ref[POOL_KEEP - k]
        s16 = s8
        for k in range(8, 16):
            s16 = s16 + stp_ref[POOL_KEEP - k]
        tot = jnp.where(lane < 64, s2, jnp.where(lane < 128, s4, jnp.where(lane < 192, s8, s16)))
        cnt = win_lane.astype(F32)
    o_c = bg * z
    pooled = tot / cnt - u
    o_b = _dot(pooled.astype(BF), pw_ref[...]) * ps_ref[...]

    branches = [oa_ref[0], o_b, o_c, od_ref[0]]
    mixed = jnp.zeros((tm, D_MODEL), F32)
    for n in range(N_BRANCH):
        pb = _dot(branches[n].astype(BF), wb_ref[n])
        gate = jax.nn.sigmoid(_dot(h, wg_ref[:, n * D_MODEL:(n + 1) * D_MODEL]))
        mixed = mixed + gate * pb
    xo_ref[0] = x + gt_ref[0, 0, 0] * _dot(mixed.astype(BF), wo_ref[...])


def _merge_call(l, x3, mod, g, proj, prev_c, prev_p, o_a, o_d, wg, wb, wo, pw, ps, cw, tm, halo):
    bx, tx, _ = x3.shape
    rm = mod.shape[3]
    if halo:
        nblk = tm // HALO
        prev_specs = [
            pl.BlockSpec((1, HALO, 3 * CONV_W), lambda b, i: (b, jnp.maximum(i * nblk - 1, 0), R_CV // (3 * CONV_W))),
            pl.BlockSpec((1, HALO, POOL_W), lambda b, i: (b, jnp.maximum(i * nblk - 1, 0), R_UB // POOL_W)),
        ]
    else:
        prev_specs = [pl.BlockSpec(prev_c.shape, lambda b, i: (0, 0, 0)),
                      pl.BlockSpec(prev_p.shape, lambda b, i: (0, 0, 0))]
    return pl.pallas_call(
        functools.partial(_merge_kernel, halo=halo),
        out_shape=(jax.ShapeDtypeStruct((bx, tx, D_MODEL), F32), jax.ShapeDtypeStruct((bx, tx, CONV_W), F32)),
        grid=(bx, tx // tm),
        in_specs=[
            pl.BlockSpec((1, tm, D_MODEL), lambda b, i: (b, i, 0)),
            _mod_spec(l, 1, rm), _mod_spec(l, 0, rm), _mod_spec(l, 2, rm),
            _const_spec((1, D_MODEL)),
            pl.BlockSpec((1, tm, 3 * CONV_W), lambda b, i: (b, i, R_CV // (3 * CONV_W))),
            pl.BlockSpec((1, tm, POOL_W), lambda b, i: (b, i, R_UB // POOL_W)),
            *prev_specs,
            pl.BlockSpec((1, tm, BRANCH_W), lambda b, i: (b, i, 0)),
            pl.BlockSpec((1, tm, BRANCH_W), lambda b, i: (b, i, 0)),
            _const_spec(wg.shape), _const_spec(wb.shape), _const_spec(wo.shape),
            _const_spec(pw.shape), _const_spec(ps.shape), _const_spec(cw.shape),
        ],
        out_specs=(pl.BlockSpec((1, tm, D_MODEL), lambda b, i: (b, i, 0)),
                   pl.BlockSpec((1, tm, CONV_W), lambda b, i: (b, i, 0))),
        compiler_params=_cparams(("arbitrary", "arbitrary")),
    )(x3, mod, mod, mod, g, proj, proj, prev_c, prev_p, o_a, o_d, wg, wb, wo, pw, ps, cw)


def _ffn_kernel(*refs, halo, final, keep_rows):
    (x_ref, sc_ref, sh_ref, gt_ref, g_ref, prev_ref, wu_ref, fc_ref, wd_ref, fg_ref, xo_ref, up_ref) = refs
    i = pl.program_id(1)
    x = x_ref[0]
    tm = x.shape[0]
    g = g_ref[...]
    sc = sc_ref[0, 0, 0]
    sh = sh_ref[0, 0, 0]
    if halo:
        hx = jnp.concatenate([prev_ref[0], x], axis=0)
        h2 = _norm_mod(hx, g, sc, sh).astype(BF)
        rows = lax.broadcasted_iota(jnp.int32, (tm + FFN_HALO, 1), 0)
        live = jnp.where(rows >= FFN_HALO, 1.0, jnp.where(i > 0, 1.0, 0.0))
    else:
        h2 = _norm_mod(x, g, sc, sh).astype(BF)
    acc = jnp.zeros((tm, D_MODEL), F32)
    for c in range(D_FF // FF_CHUNK):
        halves = []
        for part in range(2):
            col = part * D_FF + c * FF_CHUNK
            up = _dot(h2, wu_ref[:, col:col + FF_CHUNK])
            w = fc_ref[:, col:col + FF_CHUNK]
            if halo:
                up = up * live
                up_ref[0, :, col:col + FF_CHUNK] = up[FFN_HALO + tm - keep_rows:]
                upc = (w[0:1] * pltpu.roll(up, 2, axis=0) + w[1:2] * pltpu.roll(up, 1, axis=0)
                       + w[2:3] * up)[FFN_HALO:]
            else:
                up_ref[0, :, col:col + FF_CHUNK] = up
                upc = (w[0:1] * prev_ref[0, :, col:col + FF_CHUNK] + w[1:2] * prev_ref[1, :, col:col + FF_CHUNK]
                       + w[2:3] * up)
            halves.append(upc)
        val, gg = halves
        act = (gg * jax.nn.sigmoid(gg) * val).astype(BF)
        acc = acc + _dot(act, wd_ref[c * FF_CHUNK:(c + 1) * FF_CHUNK, :])
    out = x + gt_ref[0, 0, 0] * acc
    if final:
        ms = jnp.mean(out * out, axis=-1, keepdims=True)
        out = out * lax.rsqrt(ms + EPS) * fg_ref[...]
    xo_ref[0] = out


def _ffn_call(l, x3, mod, g, prev, wu, fc, wd, fg, tm, halo, final):
    bx, tx, _ = x3.shape
    rm = mod.shape[3]
    keep_rows = SUBLANES if halo else tm
    if halo:
        nblk = tm // FFN_HALO
        prev_spec = pl.BlockSpec((1, FFN_HALO, D_MODEL), lambda b, i: (b, jnp.maximum(i * nblk - 1, 0), 0))
    else:
        prev_spec = pl.BlockSpec(prev.shape, lambda b, i: (0, 0, 0))
    return pl.pallas_call(
        functools.partial(_ffn_kernel, halo=halo, final=final, keep_rows=keep_rows),
        out_shape=(jax.ShapeDtypeStruct((bx, tx, D_MODEL), F32),
                   jax.ShapeDtypeStruct((bx, keep_rows, 2 * D_FF), F32)),
        grid=(bx, tx // tm),
        in_specs=[
            pl.BlockSpec((1, tm, D_MODEL), lambda b, i: (b, i, 0)),
            _mod_spec(l, 4, rm), _mod_spec(l, 3, rm), _mod_spec(l, 5, rm),
            _const_spec((1, D_MODEL)),
            prev_spec,
            _const_spec(wu.shape), _const_spec(fc.shape), _const_spec(wd.shape), _const_spec((1, D_MODEL)),
        ],
        out_specs=(pl.BlockSpec((1, tm, D_MODEL), lambda b, i: (b, i, 0)),
                   pl.BlockSpec((1, keep_rows, 2 * D_FF), lambda b, i: (b, 0, 0))),
        compiler_params=_cparams(("arbitrary", "arbitrary")),
    )(x3, mod, mod, mod, g, prev, wu, fc, wd, fg)


def _roll_rows_kernel(state_ref, new_ref, o_ref):
    rows = state_ref.shape[2]
    last = lax.broadcasted_iota(jnp.int32, (rows, 1), 0) == rows - 1
    for s in range(state_ref.shape[1]):
        shifted = pltpu.roll(state_ref[0, s], rows - 1, axis=0)
        o_ref[0, s] = jnp.where(last, new_ref[0, s], shifted)


def _roll_rows_call(state, new):
    depth, bs, rows, w = state.shape
    return pl.pallas_call(
        _roll_rows_kernel,
        out_shape=jax.ShapeDtypeStruct(state.shape, state.dtype),
        grid=(depth, bs // SUBLANES),
        in_specs=[pl.BlockSpec((1, SUBLANES, rows, w), lambda l, b: (l, b, 0, 0)),
                  pl.BlockSpec((1, SUBLANES, 1, w), lambda l, b: (l, b, 0, 0))],
        out_specs=pl.BlockSpec((1, SUBLANES, rows, w), lambda l, b: (l, b, 0, 0)),
        compiler_params=_cparams(("arbitrary", "arbitrary")),
    )(state, new)


def _pack_w_in(w_in):
    o = np.cumsum([0, 256, KV_W, WIN_W, GA_W, POOL_W, 3 * CONV_W, 256, N_BRANCH * D_MODEL])
    q_a, kv, win, g_a, u_b, cv, q_d, gates = [w_in[..., o[k]:o[k + 1]] for k in range(8)]
    g_a = jnp.pad(g_a, ((0, 0), (0, 0), (0, LANES - GA_W)))
    packed = jnp.concatenate([kv, cv, q_a, u_b, q_d, win, g_a], axis=-1).astype(BF)
    return packed, gates.astype(BF)


def _compress_weights(cmp_pe, cmp_w1, cmp_w2):
    depth = cmp_w1.shape[0]
    half = CMP_STRIDE * NSA_HD
    w1 = cmp_w1.reshape(depth, 2, 2, CMP_STRIDE, NSA_HD, CMP_HID)
    z = jnp.zeros((depth, CMP_STRIDE, NSA_HD, CMP_HID), cmp_w1.dtype)
    cols = []
    for kv in range(2):
        for hf in range(2):
            blk = w1[:, kv, hf]
            pair = (blk, z) if kv == 0 else (z, blk)
            cols.append(jnp.concatenate(pair, axis=2).reshape(depth, 2 * half, CMP_HID))
    wc = jnp.concatenate(cols, axis=-1).astype(BF)
    pe = cmp_pe.reshape(depth, 2, 2, CMP_STRIDE, NSA_HD)
    pe2 = jnp.concatenate([pe[:, 0], pe[:, 1]], axis=-1).reshape(depth, 2, 2 * half)
    pe2 = jnp.pad(pe2, ((0, 0), (0, SUBLANES - 2), (0, 0))).astype(BF)
    zz = jnp.zeros((depth, CMP_HID, NSA_HD), cmp_w2.dtype)
    w2 = jnp.concatenate([jnp.concatenate([cmp_w2[:, 0], zz], axis=-1),
                          jnp.concatenate([zz, cmp_w2[:, 1]], axis=-1)], axis=1).astype(BF)
    return wc, pe2, w2


def _overlap(n_cmp, n_sel):
    cs = np.arange(n_cmp)[:, None] * CMP_STRIDE
    ss = np.arange(n_sel)[None, :] * SEL_BLOCK
    ov = np.minimum(cs + CMP_BLOCK, ss + SEL_BLOCK) - np.maximum(cs, ss)
    return (np.maximum(ov, 0) / CMP_BLOCK).astype(np.float32)


def _expand_matrix(n_rows, n_pos):
    return (np.arange(n_pos)[None, :] // SEL_BLOCK == np.arange(n_rows)[:, None]).astype(np.float32)


def _bf16_terms(x, n):
    terms = []
    for _ in range(n):
        u = np.array(x, np.float32).view(np.uint32)
        t = ((u + ((u >> 16) & 1) + 0x7FFF) & np.uint32(0xFFFF0000)).view(np.float32)
        terms.append(float(t))
        x = x - float(t)
    return terms


def _aux_key_table(n_pos):
    pos = np.arange(n_pos)
    tab = np.zeros((n_pos, LANES), np.float32)
    tab[:, 0:AUX_TERMS] = (pos // SEL_BLOCK)[:, None]
    tab[:, AUX_TERMS:2 * AUX_TERMS] = (pos % SEL_BLOCK)[:, None]
    tab[pos, AUX_SEL_ROW + pos // SEL_BLOCK] = 1.0
    return tab


def _aux_query_rows():
    terms = _bf16_terms(LOG2E, AUX_TERMS)
    slopes = _slopes(NSA_HEADS)
    slab_slope = [[slopes[h] for h in range(4)], [slopes[sl // 2] for sl in range(4)],
                  [slopes[2 + sl // 2] for sl in range(4)]]
    out = np.zeros((3, LANES, 4 * QBLOCK), np.float32)
    for g in range(3):
        for sl in range(4):
            for k, c in enumerate(terms):
                out[g, k, sl * QBLOCK:(sl + 1) * QBLOCK] = SEL_BLOCK * slab_slope[g][sl] * c
                out[g, AUX_TERMS + k, sl * QBLOCK:(sl + 1) * QBLOCK] = slab_slope[g][sl] * c
    return out


def _group_mean_matrix():
    idx = np.arange(DIFF_HEADS * DIFF_V) // DIFF_V
    return (idx[:, None] == idx[None, :]).astype(np.float32) / DIFF_V


def _pool_blockdiag(pool_w):
    depth, n, g, _ = pool_w.shape
    out = jnp.zeros((depth, n * g, n * g), pool_w.dtype)
    for k in range(n):
        out = out.at[:, k * g:(k + 1) * g, k * g:(k + 1) * g].set(pool_w[:, k])
    return out.astype(BF)


def kernel(x_prompt, x_sample, cache_kv, state_win_kv, state_pool, state_conv, state_ffn, page_table, c_prompt, c_sample, norm_g, ada_w, ada_b, w_in, cmp_pe, cmp_w1, cmp_w2, diff_lam, diff_norm_g, pool_w, pool_scale, conv_w, w_branch, w_out, w_up, ffn_conv, w_down, final_g):
    depth = w_in.shape[0]
    bp, seq, _ = x_prompt.shape
    bs = x_sample.shape[0]
    n_pages = page_table.shape[1]
    past = n_pages * PAGE_SIZE
    tm = min(ROW_TILE, seq)
    assert seq % tm == 0 and seq % QBLOCK == 0 and seq >= WINDOW + QBLOCK and x_sample.shape[1] == 1

    w_packed, w_gate = _pack_w_in(w_in)
    wc, pe2, w2 = _compress_weights(cmp_pe, cmp_w1, cmp_w2)
    wb = w_branch.astype(BF)
    wo = w_out.astype(BF)
    wu = w_up.astype(BF)
    wd = w_down.astype(BF)
    pw = _pool_blockdiag(pool_w)
    ps = pool_scale.reshape(depth, 1, POOL_W)
    gain = jnp.tile(diff_norm_g, (1, DIFF_HEADS)).reshape(depth, 1, DIFF_HEADS * DIFF_V)
    fg = final_g.reshape(1, D_MODEL)
    gmat = jnp.asarray(_group_mean_matrix())

    n_ch = seq // CMP_STRIDE
    n_sel_p = seq // SEL_BLOCK
    ov_p = np.zeros((n_sel_p, n_ch), np.float32)
    ov_p[:, :n_ch - 1] = _overlap(n_ch - 1, n_sel_p).T
    ov_p = jnp.asarray(ov_p)
    aux_p = jnp.asarray(_aux_key_table(seq), dtype=BF)
    qaux = jnp.asarray(_aux_query_rows())
    n_sel_s = -(-(past + 1) // SEL_BLOCK)
    n_chp = past // CMP_STRIDE
    ov_s = np.zeros((n_chp, LANES), np.float32)
    ov_s[:, :n_sel_s] = _overlap(n_chp, n_sel_s)
    ov_s = jnp.asarray(ov_s)
    emat_s = jnp.asarray(_expand_matrix(LANES, past), dtype=BF)

    mod = _ada_call(jnp.concatenate([c_prompt, c_sample], axis=0), ada_w, ada_b)
    mod_p = mod[:, :bp].reshape(depth, bp, 6, 1, D_MODEL)
    mod_s = jnp.transpose(mod[:, bp:].reshape(depth, bs, 6, D_MODEL), (0, 2, 1, 3)).reshape(depth, 1, 6, bs, D_MODEL)

    x = x_prompt
    kv_p, win_p, pool_p, conv_p, ffn_p = [], [], [], [], []
    for l in range(depth):
        lam_init = 0.8 - 0.6 * math.exp(-0.3 * l)
        g1 = norm_g[l, 0].reshape(1, D_MODEL)
        g2 = norm_g[l, 1].reshape(1, D_MODEL)
        kv32, rest, kvb, kvt = _proj_call(l, x, mod_p, g1, w_packed[l], tm)
        chunks = kvb[:, :, 0:2 * NSA_HD].reshape(bp, n_ch, CMP_STRIDE * 2 * NSA_HD)
        kcvc = _compress_call(chunks, wc[l], pe2[l], w2[l])
        o_a, o_d = _prompt_attn_call(l, rest, kvb, kvt, kcvc, jnp.swapaxes(kcvc, 1, 2),
                                     ov_p, aux_p, qaux, gmat, diff_lam, gain, lam_init)
        x, zin = _merge_call(l, x, mod_p, g1, rest, rest, rest, o_a, o_d, w_gate[l], wb[l], wo[l], pw[l], ps[l],
                             conv_w[l], tm, True)
        x, up_last = _ffn_call(l, x, mod_p, g2, x, wu[l], ffn_conv[l], wd[l], fg, tm, True, l == depth - 1)
        kv_p.append(kv32)
        win_p.append(rest[:, seq - min(WINDOW, seq):, R_WIN:R_WIN + WIN_W])
        pool_p.append(rest[:, seq - POOL_KEEP:, R_UB:R_UB + POOL_W])
        conv_p.append(zin[:, seq - 2:])
        ffn_p.append(up_last[:, SUBLANES - 2:])
    y_prompt = x

    x = x_sample.reshape(1, bs, D_MODEL)
    kv_s, win_s, pool_s, conv_s, ffn_s = [], [], [], [], []
    for l in range(depth):
        lam_init = 0.8 - 0.6 * math.exp(-0.3 * l)
        g1 = norm_g[l, 0].reshape(1, D_MODEL)
        g2 = norm_g[l, 1].reshape(1, D_MODEL)
        kv32, rest, _, _ = _proj_call(l, x, mod_s, g1, w_packed[l], bs)
        o_a, o_d = _sample_attn_call(l, page_table, kv32.reshape(bs, 1, KV_W), rest.reshape(bs, 1, R_W), cache_kv,
                                     state_win_kv, wc[l], pe2[l], w2[l], ov_s, emat_s, gmat, diff_lam, gain,
                                     lam_init, n_sel_s)
        st_conv = jnp.transpose(state_conv[l], (1, 0, 2))
        st_pool = jnp.transpose(state_pool[l], (1, 0, 2))
        st_ffn = jnp.transpose(state_ffn[l], (1, 0, 2))
        x, zin = _merge_call(l, x, mod_s, g1, rest, st_conv, st_pool, o_a.reshape(1, bs, BRANCH_W),
                             o_d.reshape(1, bs, BRANCH_W), w_gate[l], wb[l], wo[l], pw[l], ps[l], conv_w[l], bs, False)
        x, up_new = _ffn_call(l, x, mod_s, g2, st_ffn, wu[l], ffn_conv[l], wd[l], fg, bs, False, l == depth - 1)
        kv_s.append(kv32.reshape(bs, 1, KV_W))
        win_s.append(rest[0, :, R_WIN:R_WIN + WIN_W].reshape(bs, 1, WIN_W))
        pool_s.append(rest[0, :, R_UB:R_UB + POOL_W].reshape(bs, 1, POOL_W))
        conv_s.append(zin.reshape(bs, 1, CONV_W))
        ffn_s.append(up_new.reshape(bs, 1, 2 * D_FF))
    y_sample = x.reshape(bs, 1, D_MODEL)

    def roll_state(state, new_rows, keep):
        new = jnp.stack(new_rows)
        if keep == state.shape[2] and keep % SUBLANES == 0 and state.shape[1] % SUBLANES == 0:
            return _roll_rows_call(state, new)
        return jnp.concatenate([state[:, :, state.shape[2] - (keep - 1):], new], axis=2)

    return (y_prompt, y_sample, jnp.stack(kv_p), jnp.stack(win_p), jnp.stack(pool_p), jnp.stack(conv_p),
            jnp.stack(ffn_p), jnp.stack(kv_s),
            roll_state(state_win_kv, win_s, min(WINDOW, past + 1)), roll_state(state_pool, pool_s, POOL_KEEP),
            roll_state(state_conv, conv_s, 2), roll_state(state_ffn, ffn_s, 2))
```
